```python
import jax, jax.numpy as jnp
from jax import lax
import numpy as np

D_MODEL = 1024
BATCH = 16
SEQ = 2048
DEPTH = 2

D_FF = ((8 * D_MODEL // 3 + 255) // 256) * 256
D_CONV = D_MODEL // 2
CONV_WIDTH = 31
D_SHORT = D_MODEL // 4
SHORT_WIDTH = 3
D_POOL = D_MODEL // 4
POOL_WINDOWS = (2, 4, 8, 16)
POOL_GROUPS = len(POOL_WINDOWS)
POOL_GROUP_DIM = D_POOL // POOL_GROUPS
N_BRANCH = 3
IN_COLS = 2 * D_CONV + 3 * D_SHORT + D_POOL + N_BRANCH * D_MODEL
EPS = 1e-6

kernel_name = "hybrid_gated_conv_shortconv_pool_macaron"

_SPLITS = list(np.cumsum([D_CONV, D_CONV, D_SHORT, D_SHORT, D_SHORT, D_POOL]).tolist())


def rms_norm(x, g):
    xf = x.astype(jnp.float32)
    y = xf * lax.rsqrt(jnp.mean(xf * xf, axis=-1, keepdims=True) + EPS)
    return (y * g.astype(jnp.float32)).astype(x.dtype)


def layer_norm(x, g, b):
    xf = x.astype(jnp.float32)
    mu = jnp.mean(xf, axis=-1, keepdims=True)
    xc = xf - mu
    y = xc * lax.rsqrt(jnp.mean(xc * xc, axis=-1, keepdims=True) + EPS)
    return (y * g.astype(jnp.float32) + b.astype(jnp.float32)).astype(x.dtype)


def swiglu(h, w_gate, w_up, w_down):
    return (jax.nn.silu(h @ w_gate) * (h @ w_up)) @ w_down


def causal_depthwise_conv(u, w):
    k_width, channels = w.shape
    return lax.conv_general_dilated(
        u, w[:, None, :].astype(u.dtype),
        window_strides=(1,), padding=[(k_width - 1, 0)],
        dimension_numbers=("NWC", "WIO", "NWC"),
        feature_group_count=channels)


def multiscale_causal_pool(u):
    seq = u.shape[1]
    uf = u.astype(jnp.float32)
    cs = jnp.cumsum(uf, axis=1)
    count_base = jnp.arange(1, seq + 1, dtype=jnp.float32)[None, :, None]
    means = []
    for g, w in enumerate(POOL_WINDOWS):
        csg = cs[..., g * POOL_GROUP_DIM:(g + 1) * POOL_GROUP_DIM]
        lagged = jnp.pad(csg, ((0, 0), (w, 0), (0, 0)))[:, :seq]
        means.append((csg - lagged) / jnp.minimum(count_base, float(w)))
    return (jnp.concatenate(means, axis=-1) - uf).astype(u.dtype)


def hybrid_mixer(h, w_in, conv_dw, conv_b, conv_ln_g, conv_ln_b, w_pa,
                 short_dw, w_pb, pool_w, pool_scale, w_pc, w_o):
    bsz, seq, _ = h.shape
    u = h @ w_in
    a_val, a_gate, b_gate, c_gate, b_x, p_in, gate_logits = jnp.split(u, _SPLITS, axis=-1)

    a = a_val * jax.nn.sigmoid(a_gate)
    a = causal_depthwise_conv(a, conv_dw) + conv_b
    a = jax.nn.silu(layer_norm(a, conv_ln_g, conv_ln_b))
    y_a = a @ w_pa

    s = causal_depthwise_conv(c_gate * b_x, short_dw)
    y_b = (b_gate * s) @ w_pb

    p = multiscale_causal_pool(p_in).reshape(bsz, seq, POOL_GROUPS, POOL_GROUP_DIM)
    p = jnp.einsum("bsgc,gcd->bsgd", p, pool_w).reshape(bsz, seq, D_POOL) * pool_scale
    y_c = p @ w_pc

    g = jax.nn.sigmoid(gate_logits).reshape(bsz, seq, N_BRANCH, D_MODEL)
    merged = g[..., 0, :] * y_a + g[..., 1, :] * y_b + g[..., 2, :] * y_c
    return merged @ w_o


def setup_inputs(seed: int = 0) -> dict:
    key = jax.random.key(seed)
    ks = jax.random.split(key, 24)

    def dense(k, shape):
        return jax.random.normal(k, shape, jnp.float32) * (shape[-2] ** -0.5)

    def gain(k, shape, s=0.05):
        return 1.0 + s * jax.random.normal(k, shape, jnp.float32)

    def small(k, shape, s=0.02):
        return s * jax.random.normal(k, shape, jnp.float32)

    return {
        "x": jax.random.normal(ks[0], (BATCH, SEQ, D_MODEL), jnp.float32),
        "norm_ffn1_g": gain(ks[1], (DEPTH, D_MODEL)),
        "ffn1_w_gate": dense(ks[2], (DEPTH, D_MODEL, D_FF)),
        "ffn1_w_up": dense(ks[3], (DEPTH, D_MODEL, D_FF)),
        "ffn1_w_down": dense(ks[4], (DEPTH, D_FF, D_MODEL)),
        "norm_mix_g": gain(ks[5], (DEPTH, D_MODEL)),
        "w_in": dense(ks[6], (DEPTH, D_MODEL, IN_COLS)),
        "conv_dw": jax.random.normal(ks[7], (DEPTH, CONV_WIDTH, D_CONV), jnp.float32) * (CONV_WIDTH ** -0.5),
        "conv_b": small(ks[8], (DEPTH, D_CONV)),
        "conv_ln_g": gain(ks[9], (DEPTH, D_CONV)),
        "conv_ln_b": small(ks[10], (DEPTH, D_CONV)),
        "w_pa": dense(ks[11], (DEPTH, D_CONV, D_MODEL)),
        "short_dw": jax.random.normal(ks[12], (DEPTH, SHORT_WIDTH, D_SHORT), jnp.float32) * (SHORT_WIDTH ** -0.5),
        "w_pb": dense(ks[13], (DEPTH, D_SHORT, D_MODEL)),
        "pool_w": dense(ks[14], (DEPTH, POOL_GROUPS, POOL_GROUP_DIM, POOL_GROUP_DIM)),
        "pool_scale": gain(ks[15], (DEPTH, D_POOL), 0.1),
        "w_pc": dense(ks[16], (DEPTH, D_POOL, D_MODEL)),
        "w_o": dense(ks[17], (DEPTH, D_MODEL, D_MODEL)),
        "norm_ffn2_g": gain(ks[18], (DEPTH, D_MODEL)),
        "ffn2_w_gate": dense(ks[19], (DEPTH, D_MODEL, D_FF)),
        "ffn2_w_up": dense(ks[20], (DEPTH, D_MODEL, D_FF)),
        "ffn2_w_down": dense(ks[21], (DEPTH, D_FF, D_MODEL)),
        "final_norm_g": gain(ks[22], (D_MODEL,)),
    }


def reference(x, norm_ffn1_g, ffn1_w_gate, ffn1_w_up, ffn1_w_down, norm_mix_g, w_in,
              conv_dw, conv_b, conv_ln_g, conv_ln_b, w_pa, short_dw, w_pb, pool_w,
              pool_scale, w_pc, w_o, norm_ffn2_g, ffn2_w_gate, ffn2_w_up, ffn2_w_down,
              final_norm_g):
    for l in range(DEPTH):
        x = x + 0.5 * swiglu(rms_norm(x, norm_ffn1_g[l]), ffn1_w_gate[l], ffn1_w_up[l], ffn1_w_down[l])
        x = x + hybrid_mixer(rms_norm(x, norm_mix_g[l]), w_in[l], conv_dw[l], conv_b[l],
                             conv_ln_g[l], conv_ln_b[l], w_pa[l], short_dw[l], w_pb[l],
                             pool_w[l], pool_scale[l], w_pc[l], w_o[l])
        x = x + 0.5 * swiglu(rms_norm(x, norm_ffn2_g[l]), ffn2_w_gate[l], ffn2_w_up[l], ffn2_w_down[l])
    return rms_norm(x, final_norm_g)
```

```python
import functools

import jax
import jax.numpy as jnp
from jax.experimental import pallas as pl
from jax.experimental.pallas import tpu as pltpu

EPS = 1e-6
POOL_WINDOWS = (2, 4, 8, 16)

V7X_SUBLANES = 8
V7X_VMEM_BYTES = 64 * 1024 * 1024

FFN_ROWS = 512
FFN_COLS = 256
MIX_ROWS = 512
MIX_COLS = 256
CONV_ROW_BLOCK = 32
CONV_HALO = 32
SHORT_HALO = 8
POOL_HALO = 16
VMEM_LIMIT = 56 * 1024 * 1024

_F32 = jnp.float32
_BF16 = jnp.bfloat16


def _rms_norm(x, g):
    return x * jax.lax.rsqrt(jnp.mean(x * x, axis=-1, keepdims=True) + EPS) * g


def _dot(a, b):
    return jnp.dot(a, b, preferred_element_type=_F32)


def _resident(shape):
    return pl.BlockSpec(shape, lambda *_: (0,) * len(shape), pipeline_mode=pl.Buffered(1))


def _ffn_kernel(x_ref, g_ref, wg_ref, wu_ref, wd_ref, fg_ref, o_ref, act_ref, *, final_norm):
    x = x_ref[...]
    h = _rms_norm(x, g_ref[...]).astype(_BF16)
    d_ff = wg_ref.shape[1]
    for c in range(0, d_ff, FFN_COLS):
        gate = _dot(h, wg_ref[:, c:c + FFN_COLS])
        up = _dot(h, wu_ref[:, c:c + FFN_COLS])
        act_ref[:, c:c + FFN_COLS] = (gate * jax.nn.sigmoid(gate) * up).astype(_BF16)
    out = x + 0.5 * _dot(act_ref[...], wd_ref[...])
    if final_norm:
        out = _rms_norm(out, fg_ref[...])
    o_ref[...] = out


def _ffn(x, g, wg, wu, wd, fg, *, final_norm):
    m, d = x.shape
    d_ff = wg.shape[1]
    return pl.pallas_call(
        functools.partial(_ffn_kernel, final_norm=final_norm),
        grid=(m // FFN_ROWS,),
        in_specs=[
            pl.BlockSpec((FFN_ROWS, d), lambda i: (i, 0)),
            _resident((1, d)),
            _resident((d, d_ff)),
            _resident((d, d_ff)),
            _resident((d_ff, d)),
            _resident((1, d)),
        ],
        out_specs=pl.BlockSpec((FFN_ROWS, d), lambda i: (i, 0)),
        out_shape=jax.ShapeDtypeStruct((m, d), _F32),
        scratch_shapes=[pltpu.VMEM((FFN_ROWS, d_ff), _BF16)],
        compiler_params=pltpu.CompilerParams(
            dimension_semantics=("arbitrary",), vmem_limit_bytes=VMEM_LIMIT),
        name="swiglu_half_step",
    )(x, g, wg, wu, wd, fg)


def _causal_taps(buf_ref, w_ref, halo, rows, row_block):
    k_width, channels = w_ref.shape
    first = halo - (k_width - 1)
    blocks = []
    for r in range(0, rows, row_block):
        acc = jnp.zeros((row_block, channels), _F32)
        for k in range(k_width):
            acc = acc + w_ref[k:k + 1, :] * buf_ref[pl.ds(first + r + k, row_block), :]
        blocks.append(acc)
    return jnp.concatenate(blocks, axis=0)


def _mixer_kernel(x_ref, g_ref, win_ref, cdw_ref, cb_ref, lng_ref, lnb_ref, wpa_ref, sdw_ref,
                  wpb_ref, pw_ref, ps_ref, wpc_ref, wo_ref, o_ref,
                  a_buf, v_buf, p_buf, merged_ref):
    rows, d_model = x_ref.shape
    d_conv = wpa_ref.shape[0]
    d_short = wpb_ref.shape[0]
    d_pool = wpc_ref.shape[0]
    step = pl.program_id(1)

    @pl.when(step == 0)
    def _():
        a_buf[0:CONV_HALO, :] = jnp.zeros((CONV_HALO, d_conv), _F32)
        v_buf[0:SHORT_HALO, :] = jnp.zeros((SHORT_HALO, d_short), _F32)
        p_buf[0:POOL_HALO, :] = jnp.zeros((POOL_HALO, d_pool), _F32)

    x = x_ref[...]
    h = _rms_norm(x, g_ref[...]).astype(_BF16)

    def proj(c0, width):
        return _dot(h, win_ref[:, c0:c0 + width])

    c_aval = 0
    c_agate = c_aval + d_conv
    c_bgate = c_agate + d_conv
    c_cgate = c_bgate + d_short
    c_bx = c_cgate + d_short
    c_pin = c_bx + d_short
    c_gates = c_pin + d_pool

    a_buf[CONV_HALO:CONV_HALO + rows, :] = proj(c_aval, d_conv) * jax.nn.sigmoid(proj(c_agate, d_conv))
    conv = _causal_taps(a_buf, cdw_ref, CONV_HALO, rows, CONV_ROW_BLOCK) + cb_ref[...]
    a_buf[0:CONV_HALO, :] = a_buf[rows:rows + CONV_HALO, :]
    mu = jnp.mean(conv, axis=-1, keepdims=True)
    xc = conv - mu
    ln = xc * jax.lax.rsqrt(jnp.mean(xc * xc, axis=-1, keepdims=True) + EPS) * lng_ref[...] + lnb_ref[...]
    a_act = (ln * jax.nn.sigmoid(ln)).astype(_BF16)

    b_gate = proj(c_bgate, d_short)
    v_buf[SHORT_HALO:SHORT_HALO + rows, :] = proj(c_cgate, d_short) * proj(c_bx, d_short)
    s_conv = _causal_taps(v_buf, sdw_ref, SHORT_HALO, rows, 2 * CONV_ROW_BLOCK)
    v_buf[0:SHORT_HALO, :] = v_buf[rows:rows + SHORT_HALO, :]
    b_act = (b_gate * s_conv).astype(_BF16)

    p_in = proj(c_pin, d_pool)
    p_buf[POOL_HALO:POOL_HALO + rows, :] = p_in
    group = jax.lax.broadcasted_iota(jnp.int32, (rows, d_pool), 1) // (d_pool // len(POOL_WINDOWS))
    seen = (step * rows + 1 + jax.lax.broadcasted_iota(jnp.int32, (rows, d_pool), 0)).astype(_F32)
    wsum = p_in
    pooled = jnp.zeros((rows, d_pool), _F32)
    lag = 1
    for gi, window in enumerate(POOL_WINDOWS):
        while lag < window:
            wsum = wsum + p_buf[pl.ds(POOL_HALO - lag, rows), :]
            lag += 1
        mean = wsum / jnp.minimum(seen, float(window))
        pooled = jnp.where(group == gi, mean, pooled)
    p_buf[0:POOL_HALO, :] = p_buf[rows:rows + POOL_HALO, :]
    pooled = (pooled - p_in).astype(_BF16)
    c_act = (_dot(pooled, pw_ref[...]) * ps_ref[...]).astype(_BF16)

    for c in range(0, d_model, MIX_COLS):
        cols = slice(c, c + MIX_COLS)
        merged = jax.nn.sigmoid(proj(c_gates + c, MIX_COLS)) * _dot(a_act, wpa_ref[:, cols])
        merged = merged + jax.nn.sigmoid(proj(c_gates + d_model + c, MIX_COLS)) * _dot(b_act, wpb_ref[:, cols])
        merged = merged + jax.nn.sigmoid(proj(c_gates + 2 * d_model + c, MIX_COLS)) * _dot(c_act, wpc_ref[:, cols])
        merged_ref[:, cols] = merged.astype(_BF16)
    o_ref[...] = x + _dot(merged_ref[...], wo_ref[...])


def _mixer(x, seq, g, win, cdw, cb, lng, lnb, wpa, sdw, wpb, pw, ps, wpc, wo):
    m, d = x.shape
    steps = seq // MIX_ROWS
    d_conv, d_short, d_pool = wpa.shape[0], wpb.shape[0], wpc.shape[0]
    row_spec = pl.BlockSpec((MIX_ROWS, d), lambda b, s: (b * steps + s, 0))
    weights = (g, win, cdw, cb, lng, lnb, wpa, sdw, wpb, pw, ps, wpc, wo)
    return pl.pallas_call(
        _mixer_kernel,
        grid=(m // seq, steps),
        in_specs=[row_spec] + [_resident(w.shape) for w in weights],
        out_specs=row_spec,
        out_shape=jax.ShapeDtypeStruct((m, d), _F32),
        scratch_shapes=[
            pltpu.VMEM((CONV_HALO + MIX_ROWS, d_conv), _F32),
            pltpu.VMEM((SHORT_HALO + MIX_ROWS, d_short), _F32),
            pltpu.VMEM((POOL_HALO + MIX_ROWS, d_pool), _F32),
            pltpu.VMEM((MIX_ROWS, d), _BF16),
        ],
        compiler_params=pltpu.CompilerParams(
            dimension_semantics=("arbitrary", "arbitrary"), vmem_limit_bytes=VMEM_LIMIT),
        name="hybrid_mixer",
    )(x, *weights)


def kernel(x, norm_ffn1_g, ffn1_w_gate, ffn1_w_up, ffn1_w_down, norm_mix_g, w_in, conv_dw, conv_b,
           conv_ln_g, conv_ln_b, w_pa, short_dw, w_pb, pool_w, pool_scale, w_pc, w_o, norm_ffn2_g,
           ffn2_w_gate, ffn2_w_up, ffn2_w_down, final_norm_g):
    batch, seq, d_model = x.shape
    depth = w_in.shape[0]
    assert seq % MIX_ROWS == 0 and (batch * seq) % FFN_ROWS == 0
    assert MIX_ROWS >= CONV_HALO >= conv_dw.shape[1] - 1
    assert SHORT_HALO >= short_dw.shape[1] - 1 and POOL_HALO >= max(POOL_WINDOWS) - 1

    def row(v):
        return v.reshape(1, -1)

    def bf(w):
        return w.astype(_BF16)

    y = x.reshape(batch * seq, d_model)
    fg = row(final_norm_g)
    for l in range(depth):
        y = _ffn(y, row(norm_ffn1_g[l]), bf(ffn1_w_gate[l]), bf(ffn1_w_up[l]), bf(ffn1_w_down[l]), fg,
                 final_norm=False)
        pool_bd = jax.scipy.linalg.block_diag(*[pool_w[l, i] for i in range(pool_w.shape[1])])
        y = _mixer(y, seq, row(norm_mix_g[l]), bf(w_in[l]), conv_dw[l], row(conv_b[l]),
                   row(conv_ln_g[l]), row(conv_ln_b[l]), bf(w_pa[l]), short_dw[l], bf(w_pb[l]),
                   bf(pool_bd), row(pool_scale[l]), bf(w_pc[l]), bf(w_o[l]))
        y = _ffn(y, row(norm_ffn2_g[l]), bf(ffn2_w_gate[l]), bf(ffn2_w_up[l]), bf(ffn2_w_down[l]), fg,
                 final_norm=(l == depth - 1))
    return y.reshape(batch, seq, d_model)
```

```python
import functools

import jax
import jax.numpy as jnp
from jax.experimental import pallas as pl
from jax.experimental.pallas import tpu as pltpu

EPS = 1e-6
POOL_WINDOWS = (2, 4, 8, 16)

V7X_SUBLANES = 8

FFN_ROWS = 512
FFN_COLS = 256
MIX_ROWS = 512
MIX_COLS = 256
CONV_ROW_BLOCK = 32
CONV_HALO = 32
SHORT_HALO = 8
POOL_HALO = 24
VMEM_LIMIT = 56 * 1024 * 1024

_F32 = jnp.float32
_BF16 = jnp.bfloat16


def _rms_norm(x, g):
    return x * jax.lax.rsqrt(jnp.mean(x * x, axis=-1, keepdims=True) + EPS) * g


def _dot(a, b):
    return jnp.dot(a, b, preferred_element_type=_F32)


def _resident(shape):
    return pl.BlockSpec(shape, lambda *_: (0,) * len(shape), pipeline_mode=pl.Buffered(1))


def _ffn_kernel(x_ref, g_ref, wg_ref, wu_ref, wd_ref, fg_ref, o_ref, act_ref, *, final_norm):
    x = x_ref[...]
    h = _rms_norm(x, g_ref[...]).astype(_BF16)
    d_ff = wg_ref.shape[1]
    for c in range(0, d_ff, FFN_COLS):
        gate = _dot(h, wg_ref[:, c:c + FFN_COLS])
        up = _dot(h, wu_ref[:, c:c + FFN_COLS])
        act_ref[:, c:c + FFN_COLS] = (gate * jax.nn.sigmoid(gate) * up).astype(_BF16)
    out = x + 0.5 * _dot(act_ref[...], wd_ref[...])
    if final_norm:
        out = _rms_norm(out, fg_ref[...])
    o_ref[...] = out


def _ffn(x, g, wg, wu, wd, fg, *, final_norm):
    m, d = x.shape
    d_ff = wg.shape[1]
    return pl.pallas_call(
        functools.partial(_ffn_kernel, final_norm=final_norm),
        grid=(m // FFN_ROWS,),
        in_specs=[
            pl.BlockSpec((FFN_ROWS, d), lambda i: (i, 0)),
            _resident((1, d)),
            _resident((d, d_ff)),
            _resident((d, d_ff)),
            _resident((d_ff, d)),
            _resident((1, d)),
        ],
        out_specs=pl.BlockSpec((FFN_ROWS, d), lambda i: (i, 0)),
        out_shape=jax.ShapeDtypeStruct((m, d), _F32),
        scratch_shapes=[pltpu.VMEM((FFN_ROWS, d_ff), _BF16)],
        compiler_params=pltpu.CompilerParams(
            dimension_semantics=("arbitrary",), vmem_limit_bytes=VMEM_LIMIT),
        name="swiglu_half_step",
    )(x, g, wg, wu, wd, fg)


def _mixer_kernel(x_ref, g_ref, win_ref, cdw_ref, cb_ref, lng_ref, lnb_ref, wpa_ref, sdw_ref,
                  wpb_ref, pw_ref, ps_ref, wpc_ref, wo_ref, o_ref,
                  a_buf, a_shift, a_act_ref, v_buf, p_buf, s2_buf, s4_buf, s8_buf, merged_ref):
    rows, d_model = x_ref.shape
    d_conv = wpa_ref.shape[0]
    d_short = wpb_ref.shape[0]
    d_pool = wpc_ref.shape[0]
    sub = V7X_SUBLANES
    step = pl.program_id(1)

    @pl.when(step == 0)
    def _():
        a_buf[0:CONV_HALO, :] = jnp.zeros((CONV_HALO, d_conv), _F32)
        v_buf[0:SHORT_HALO, :] = jnp.zeros((SHORT_HALO, d_short), _F32)
        p_buf[0:POOL_HALO, :] = jnp.zeros((POOL_HALO, d_pool), _F32)
        s2_buf[0:sub, :] = jnp.zeros((sub, d_pool), _F32)
        s4_buf[0:sub, :] = jnp.zeros((sub, d_pool), _F32)

    x = x_ref[...]
    h = _rms_norm(x, g_ref[...]).astype(_BF16)

    def proj(c0, width):
        return _dot(h, win_ref[:, c0:c0 + width])

    c_aval = 0
    c_agate = c_aval + d_conv
    c_bgate = c_agate + d_conv
    c_cgate = c_bgate + d_short
    c_bx = c_cgate + d_short
    c_pin = c_bx + d_short
    c_gates = c_pin + d_pool

    a_buf[CONV_HALO:CONV_HALO + rows, :] = proj(c_aval, d_conv) * jax.nn.sigmoid(proj(c_agate, d_conv))
    span = a_shift.shape[1]
    for r in range(1, sub):
        a_shift[r - 1, :, :] = a_buf[pl.ds(r, span), :]
    k_width = cdw_ref.shape[0] // sub
    first = CONV_HALO - (k_width - 1)
    tiles = CONV_ROW_BLOCK // sub
    for rb in range(0, rows, CONV_ROW_BLOCK):
        acc = jnp.zeros((tiles, sub, d_conv), _F32)
        for k in range(k_width):
            q, r = divmod(first + k, sub)
            lo = rb + q * sub
            if r == 0:
                tap = a_buf[lo:lo + CONV_ROW_BLOCK, :]
            else:
                tap = a_shift[r - 1, lo:lo + CONV_ROW_BLOCK, :]
            acc = acc + cdw_ref[k * sub:(k + 1) * sub, :][None] * tap.reshape(tiles, sub, d_conv)
        acc = acc + cb_ref[...][None]
        xc = acc - jnp.mean(acc, axis=-1, keepdims=True)
        ln = xc * jax.lax.rsqrt(jnp.mean(xc * xc, axis=-1, keepdims=True) + EPS)
        ln = ln * lng_ref[...][None] + lnb_ref[...][None]
        act = (ln * jax.nn.sigmoid(ln)).reshape(CONV_ROW_BLOCK, d_conv)
        a_act_ref[rb:rb + CONV_ROW_BLOCK, :] = act.astype(_BF16)
    a_buf[0:CONV_HALO, :] = a_buf[rows:rows + CONV_HALO, :]

    b_gate = proj(c_bgate, d_short)
    v_buf[SHORT_HALO:SHORT_HALO + rows, :] = proj(c_cgate, d_short) * proj(c_bx, d_short)
    s_width = sdw_ref.shape[0]
    s_conv = jnp.zeros((rows, d_short), _F32)
    for k in range(s_width):
        s_conv = s_conv + sdw_ref[k:k + 1, :] * v_buf[pl.ds(SHORT_HALO - (s_width - 1) + k, rows), :]
    v_buf[0:SHORT_HALO, :] = v_buf[rows:rows + SHORT_HALO, :]
    b_act = (b_gate * s_conv).astype(_BF16)

    p_in = proj(c_pin, d_pool)
    p_buf[POOL_HALO:POOL_HALO + rows, :] = p_in
    n_ext = rows + POOL_HALO - sub
    ext = slice(sub, sub + n_ext)
    s2_buf[ext, :] = p_buf[ext, :] + p_buf[pl.ds(sub - 1, n_ext), :]
    s4_buf[ext, :] = s2_buf[ext, :] + s2_buf[pl.ds(sub - 2, n_ext), :]
    s8_buf[ext, :] = s4_buf[ext, :] + s4_buf[pl.ds(sub - 4, n_ext), :]
    p_buf[0:POOL_HALO, :] = p_buf[rows:rows + POOL_HALO, :]
    cur = slice(POOL_HALO, POOL_HALO + rows)
    s8 = s8_buf[cur, :]
    s16 = s8 + s8_buf[POOL_HALO - 8:POOL_HALO - 8 + rows, :]
    group = jax.lax.broadcasted_iota(jnp.int32, (rows, d_pool), 1) // (d_pool // len(POOL_WINDOWS))
    wsum = jnp.where(group == 0, s2_buf[cur, :],
                     jnp.where(group == 1, s4_buf[cur, :], jnp.where(group == 2, s8, s16)))
    seen = step * rows + 1 + jax.lax.broadcasted_iota(jnp.int32, (rows, d_pool), 0)
    count = jnp.minimum(seen, jnp.left_shift(2, group)).astype(_F32)
    pooled = (wsum / count - p_in).astype(_BF16)
    c_act = (_dot(pooled, pw_ref[...]) * ps_ref[...]).astype(_BF16)

    a_act = a_act_ref[...]
    for c in range(0, d_model, MIX_COLS):
        cols = slice(c, c + MIX_COLS)
        merged = jax.nn.sigmoid(proj(c_gates + c, MIX_COLS)) * _dot(a_act, wpa_ref[:, cols])
        merged = merged + jax.nn.sigmoid(proj(c_gates + d_model + c, MIX_COLS)) * _dot(b_act, wpb_ref[:, cols])
        merged = merged + jax.nn.sigmoid(proj(c_gates + 2 * d_model + c, MIX_COLS)) * _dot(c_act, wpc_ref[:, cols])
        merged_ref[:, cols] = merged.astype(_BF16)
    o_ref[...] = x + _dot(merged_ref[...], wo_ref[...])


def _mixer(x, seq, g, win, cdw, cb, lng, lnb, wpa, sdw, wpb, pw, ps, wpc, wo):
    m, d = x.shape
    steps = seq // MIX_ROWS
    d_conv, d_short, d_pool = wpa.shape[0], wpb.shape[0], wpc.shape[0]
    row_spec = pl.BlockSpec((MIX_ROWS, d), lambda b, s: (b * steps + s, 0))
    weights = (g, win, cdw, cb, lng, lnb, wpa, sdw, wpb, pw, ps, wpc, wo)
    pool_rows = POOL_HALO + MIX_ROWS
    return pl.pallas_call(
        _mixer_kernel,
        grid=(m // seq, steps),
        in_specs=[row_spec] + [_resident(w.shape) for w in weights],
        out_specs=row_spec,
        out_shape=jax.ShapeDtypeStruct((m, d), _F32),
        scratch_shapes=[
            pltpu.VMEM((CONV_HALO + MIX_ROWS, d_conv), _F32),
            pltpu.VMEM((V7X_SUBLANES - 1, CONV_HALO + MIX_ROWS - V7X_SUBLANES, d_conv), _F32),
            pltpu.VMEM((MIX_ROWS, d_conv), _BF16),
            pltpu.VMEM((SHORT_HALO + MIX_ROWS, d_short), _F32),
            pltpu.VMEM((pool_rows, d_pool), _F32),
            pltpu.VMEM((pool_rows, d_pool), _F32),
            pltpu.VMEM((pool_rows, d_pool), _F32),
            pltpu.VMEM((pool_rows, d_pool), _F32),
            pltpu.VMEM((MIX_ROWS, d), _BF16),
        ],
        compiler_params=pltpu.CompilerParams(
            dimension_semantics=("arbitrary", "arbitrary"), vmem_limit_bytes=VMEM_LIMIT),
        name="hybrid_mixer",
    )(x, *weights)


def kernel(x, norm_ffn1_g, ffn1_w_gate, ffn1_w_up, ffn1_w_down, norm_mix_g, w_in, conv_dw, conv_b,
           conv_ln_g, conv_ln_b, w_pa, short_dw, w_pb, pool_w, pool_scale, w_pc, w_o, norm_ffn2_g,
           ffn2_w_gate, ffn2_w_up, ffn2_w_down, final_norm_g):
    batch, seq, d_model = x.shape
    depth = w_in.shape[0]
    assert seq % MIX_ROWS == 0 and (batch * seq) % FFN_ROWS == 0
    assert MIX_ROWS >= CONV_HALO >= conv_dw.shape[1] - 1
    assert SHORT_HALO >= short_dw.shape[1] - 1
    assert POOL_WINDOWS == (2, 4, 8, 16) and pool_w.shape[1] == len(POOL_WINDOWS)

    def row(v):
        return v.reshape(1, -1)

    def bf(w):
        return w.astype(_BF16)

    def sublane_rows(v):
        return jnp.repeat(v, V7X_SUBLANES, axis=0)

    y = x.reshape(batch * seq, d_model)
    fg = row(final_norm_g)
    for l in range(depth):
        y = _ffn(y, row(norm_ffn1_g[l]), bf(ffn1_w_gate[l]), bf(ffn1_w_up[l]), bf(ffn1_w_down[l]), fg,
                 final_norm=False)
        pool_bd = jax.scipy.linalg.block_diag(*[pool_w[l, i] for i in range(pool_w.shape[1])])
        y = _mixer(y, seq, row(norm_mix_g[l]), bf(w_in[l]), sublane_rows(conv_dw[l]),
                   sublane_rows(row(conv_b[l])), sublane_rows(row(conv_ln_g[l])),
                   sublane_rows(row(conv_ln_b[l])), bf(w_pa[l]), short_dw[l], bf(w_pb[l]),
                   bf(pool_bd), row(pool_scale[l]), bf(w_pc[l]), bf(w_o[l]))
        y = _ffn(y, row(norm_ffn2_g[l]), bf(ffn2_w_gate[l]), bf(ffn2_w_up[l]), bf(ffn2_w_down[l]), fg,
                 final_norm=(l == depth - 1))
    return y.reshape(batch, seq, d_model)
```

```python
import functools

import jax
import jax.numpy as jnp
from jax.experimental import pallas as pl
from jax.experimental.pallas import tpu as pltpu

EPS = 1e-6
POOL_WINDOWS = (2, 4, 8, 16)

V7X_SUBLANES = 8

FFN_ROWS = 512
FFN_COLS = 256
MIX_ROWS = 512
MIX_PARTS = 2
MIX_COLS = 256
CONV_ROW_BLOCK = 32
CONV_HALO = 32
SHORT_HALO = 8
POOL_HALO = 24
VMEM_LIMIT = 56 * 1024 * 1024

_F32 = jnp.float32
_BF16 = jnp.bfloat16


def _rms_norm(x, g):
    return x * jax.lax.rsqrt(jnp.mean(x * x, axis=-1, keepdims=True) + EPS) * g


def _dot(a, b):
    return jnp.dot(a, b, preferred_element_type=_F32)


def _resident(shape):
    return pl.BlockSpec(shape, lambda *_: (0,) * len(shape), pipeline_mode=pl.Buffered(1))


def _interleave(primary, secondary):
    done = 0
    for i, unit in enumerate(primary):
        unit()
        upto = ((i + 1) * len(secondary)) // len(primary)
        for other in secondary[done:upto]:
            other()
        done = upto


def _ffn_kernel(x_ref, g_ref, wg_ref, wu_ref, wd_ref, fg_ref, o_ref, act_ref, *, final_norm):
    x = x_ref[...]
    h = _rms_norm(x, g_ref[...]).astype(_BF16)
    d_ff = wg_ref.shape[1]
    for c in range(0, d_ff, FFN_COLS):
        gate = _dot(h, wg_ref[:, c:c + FFN_COLS])
        up = _dot(h, wu_ref[:, c:c + FFN_COLS])
        act_ref[:, c:c + FFN_COLS] = (gate * jax.nn.sigmoid(gate) * up).astype(_BF16)
    act = act_ref[...]
    out = x + 0.5 * jnp.concatenate([_dot(act, wd_ref[n]) for n in range(wd_ref.shape[0])], axis=1)
    if final_norm:
        out = _rms_norm(out, fg_ref[...])
    o_ref[...] = out


def _ffn(x, g, wg, wu, wd, fg, *, final_norm):
    m, d = x.shape
    d_ff = wg.shape[1]
    return pl.pallas_call(
        functools.partial(_ffn_kernel, final_norm=final_norm),
        grid=(m // FFN_ROWS,),
        in_specs=[
            pl.BlockSpec((FFN_ROWS, d), lambda i: (i, 0)),
            _resident((1, d)),
            _resident((d, d_ff)),
            _resident((d, d_ff)),
            _resident(wd.shape),
            _resident((1, d)),
        ],
        out_specs=pl.BlockSpec((FFN_ROWS, d), lambda i: (i, 0)),
        out_shape=jax.ShapeDtypeStruct((m, d), _F32),
        scratch_shapes=[pltpu.VMEM((FFN_ROWS, d_ff), _BF16)],
        compiler_params=pltpu.CompilerParams(
            dimension_semantics=("arbitrary",), vmem_limit_bytes=VMEM_LIMIT),
        name="swiglu_half_step",
    )(x, g, wg, wu, wd, fg)


def _mixer_kernel(x_ref, g_ref, win_ref, cdw_ref, cb_ref, lng_ref, lnb_ref, wpa_ref, sdw_ref,
                  wpb_ref, pw_ref, ps_ref, wpc_ref, wo_ref, o_ref,
                  h_ref, u_ref, a_buf, a_shift, a_act_ref, v_buf, b_act_ref,
                  p_buf, s2_buf, s4_buf, s8_buf, c_act_ref, gate_ref, merged_ref):
    rows, d_model = x_ref.shape
    part = rows // MIX_PARTS
    d_conv = wpa_ref.shape[1]
    d_short = wpb_ref.shape[1]
    d_pool = wpc_ref.shape[1]
    sub = V7X_SUBLANES
    step = pl.program_id(1)

    @pl.when(step == 0)
    def _():
        a_buf[0:CONV_HALO, :] = jnp.zeros((CONV_HALO, d_conv), _F32)
        v_buf[0:SHORT_HALO, :] = jnp.zeros((SHORT_HALO, d_short), _F32)
        p_buf[0:POOL_HALO, :] = jnp.zeros((POOL_HALO, d_pool), _F32)
        s2_buf[0:sub, :] = jnp.zeros((sub, d_pool), _F32)
        s4_buf[0:sub, :] = jnp.zeros((sub, d_pool), _F32)

    c_aval = 0
    c_agate = c_aval + d_conv
    c_bgate = c_agate + d_conv
    c_cgate = c_bgate + d_short
    c_bx = c_cgate + d_short
    c_pin = c_bx + d_short
    c_gates = c_pin + d_pool

    def proj(o, c0):
        return _dot(h_ref[o:o + part, :], win_ref[c0 // MIX_COLS])

    def project_units(o):
        def norm():
            h_ref[o:o + part, :] = _rms_norm(x_ref[o:o + part, :], g_ref[...]).astype(_BF16)

        def mix_in(c):
            u_ref[o:o + part, c:c + MIX_COLS] = proj(o, c)

        def gate_logits(c):
            gate_ref[o:o + part, c:c + MIX_COLS] = proj(o, c_gates + c)

        mix = [functools.partial(mix_in, c) for c in range(0, c_gates, MIX_COLS)]
        gates = [functools.partial(gate_logits, c) for c in range(0, gate_ref.shape[1], MIX_COLS)]
        return [norm] + mix, gates

    k_width = cdw_ref.shape[0] // sub
    first = CONV_HALO - (k_width - 1)
    tiles = CONV_ROW_BLOCK // sub
    span = a_shift.shape[1]

    def mixing_units(o):
        rws = slice(o, o + part)

        def glu():
            a_buf[CONV_HALO + o:CONV_HALO + o + part, :] = (
                u_ref[rws, c_aval:c_aval + d_conv] * jax.nn.sigmoid(u_ref[rws, c_agate:c_agate + d_conv]))

        def shift(r):
            a_shift[r - 1, :, :] = a_buf[pl.ds(o + r, span), :]

        def conv_block(rb):
            acc = jnp.zeros((tiles, sub, d_conv), _F32)
            for k in range(k_width):
                q, r = divmod(first + k, sub)
                lo = rb + q * sub
                if r == 0:
                    tap = a_buf[o + lo:o + lo + CONV_ROW_BLOCK, :]
                else:
                    tap = a_shift[r - 1, lo:lo + CONV_ROW_BLOCK, :]
                acc = acc + cdw_ref[k * sub:(k + 1) * sub, :][None] * tap.reshape(tiles, sub, d_conv)
            acc = acc + cb_ref[...][None]
            xc = acc - jnp.mean(acc, axis=-1, keepdims=True)
            ln = xc * jax.lax.rsqrt(jnp.mean(xc * xc, axis=-1, keepdims=True) + EPS)
            ln = ln * lng_ref[...][None] + lnb_ref[...][None]
            act = (ln * jax.nn.sigmoid(ln)).reshape(CONV_ROW_BLOCK, d_conv)
            a_act_ref[o + rb:o + rb + CONV_ROW_BLOCK, :] = act.astype(_BF16)

        def short_conv():
            s_width = sdw_ref.shape[0]
            v_buf[SHORT_HALO + o:SHORT_HALO + o + part, :] = (
                u_ref[rws, c_cgate:c_cgate + d_short] * u_ref[rws, c_bx:c_bx + d_short])
            s_conv = jnp.zeros((part, d_short), _F32)
            for k in range(s_width):
                s_conv = s_conv + sdw_ref[k:k + 1, :] * v_buf[pl.ds(SHORT_HALO + o - (s_width - 1) + k, part), :]
            b_act_ref[rws, :] = (u_ref[rws, c_bgate:c_bgate + d_short] * s_conv).astype(_BF16)

        def pool():
            p_in = u_ref[rws, c_pin:c_pin + d_pool]
            p_buf[POOL_HALO + o:POOL_HALO + o + part, :] = p_in
            lo = sub if o == 0 else POOL_HALO + o
            n_ext = POOL_HALO + o + part - lo
            ext = slice(lo, lo + n_ext)
            s2_buf[ext, :] = p_buf[ext, :] + p_buf[pl.ds(lo - 1, n_ext), :]
            s4_buf[ext, :] = s2_buf[ext, :] + s2_buf[pl.ds(lo - 2, n_ext), :]
            s8_buf[ext, :] = s4_buf[ext, :] + s4_buf[pl.ds(lo - 4, n_ext), :]
            cur = slice(POOL_HALO + o, POOL_HALO + o + part)
            s8 = s8_buf[cur, :]
            s16 = s8 + s8_buf[POOL_HALO + o - 8:POOL_HALO + o - 8 + part, :]
            group = jax.lax.broadcasted_iota(jnp.int32, (part, d_pool), 1) // (d_pool // len(POOL_WINDOWS))
            wsum = jnp.where(group == 0, s2_buf[cur, :],
                             jnp.where(group == 1, s4_buf[cur, :], jnp.where(group == 2, s8, s16)))
            seen = step * rows + o + 1 + jax.lax.broadcasted_iota(jnp.int32, (part, d_pool), 0)
            count = jnp.minimum(seen, jnp.left_shift(2, group)).astype(_F32)
            pooled = (wsum / count - p_in).astype(_BF16)
            c_act_ref[rws, :] = (_dot(pooled, pw_ref[...]) * ps_ref[...]).astype(_BF16)

        units = [glu] + [functools.partial(shift, r) for r in range(1, sub)]
        units += [functools.partial(conv_block, rb) for rb in range(0, part, CONV_ROW_BLOCK)]
        return units + [short_conv, pool]

    def merge_units(o):
        rws = slice(o, o + part)

        def merge(c):
            cols = slice(c, c + MIX_COLS)

            def gate(branch):
                return jax.nn.sigmoid(gate_ref[rws, branch * d_model + c:branch * d_model + c + MIX_COLS])

            n = c // MIX_COLS
            merged = gate(0) * _dot(a_act_ref[rws, :], wpa_ref[n])
            merged = merged + gate(1) * _dot(b_act_ref[rws, :], wpb_ref[n])
            merged = merged + gate(2) * _dot(c_act_ref[rws, :], wpc_ref[n])
            merged_ref[rws, cols] = merged.astype(_BF16)

        def out(c):
            cols = slice(c, c + MIX_COLS)
            o_ref[rws, cols] = x_ref[rws, cols] + _dot(merged_ref[rws, :], wo_ref[c // MIX_COLS])

        chunks = range(0, d_model, MIX_COLS)
        return [functools.partial(merge, c) for c in chunks] + [functools.partial(out, c) for c in chunks]

    offsets = [i * part for i in range(MIX_PARTS)]
    mix_proj, gate_proj = zip(*[project_units(o) for o in offsets])
    for unit in mix_proj[0]:
        unit()
    for i, o in enumerate(offsets):
        mxu = list(gate_proj[i])
        if i + 1 < MIX_PARTS:
            mxu = mix_proj[i + 1] + mxu
        if i > 0:
            mxu = merge_units(offsets[i - 1]) + mxu
        _interleave(mixing_units(o), mxu)
    for unit in merge_units(offsets[-1]):
        unit()

    a_buf[0:CONV_HALO, :] = a_buf[rows:rows + CONV_HALO, :]
    v_buf[0:SHORT_HALO, :] = v_buf[rows:rows + SHORT_HALO, :]
    p_buf[0:POOL_HALO, :] = p_buf[rows:rows + POOL_HALO, :]


def _mixer(x, seq, g, win, cdw, cb, lng, lnb, wpa, sdw, wpb, pw, ps, wpc, wo):
    m, d = x.shape
    steps = seq // MIX_ROWS
    part = MIX_ROWS // MIX_PARTS
    d_conv, d_short, d_pool = wpa.shape[1], wpb.shape[1], wpc.shape[1]
    d_mix = 2 * d_conv + 3 * d_short + d_pool
    row_spec = pl.BlockSpec((MIX_ROWS, d), lambda b, s: (b * steps + s, 0))
    weights = (g, win, cdw, cb, lng, lnb, wpa, sdw, wpb, pw, ps, wpc, wo)
    pool_rows = POOL_HALO + MIX_ROWS
    return pl.pallas_call(
        _mixer_kernel,
        grid=(m // seq, steps),
        in_specs=[row_spec] + [_resident(w.shape) for w in weights],
        out_specs=row_spec,
        out_shape=jax.ShapeDtypeStruct((m, d), _F32),
        scratch_shapes=[
            pltpu.VMEM((MIX_ROWS, d), _BF16),
            pltpu.VMEM((MIX_ROWS, d_mix), _F32),
            pltpu.VMEM((CONV_HALO + MIX_ROWS, d_conv), _F32),
            pltpu.VMEM((V7X_SUBLANES - 1, CONV_HALO + part - V7X_SUBLANES, d_conv), _F32),
            pltpu.VMEM((MIX_ROWS, d_conv), _BF16),
            pltpu.VMEM((SHORT_HALO + MIX_ROWS, d_short), _F32),
            pltpu.VMEM((MIX_ROWS, d_short), _BF16),
            pltpu.VMEM((pool_rows, d_pool), _F32),
            pltpu.VMEM((pool_rows, d_pool), _F32),
            pltpu.VMEM((pool_rows, d_pool), _F32),
            pltpu.VMEM((pool_rows, d_pool), _F32),
            pltpu.VMEM((MIX_ROWS, d_pool), _BF16),
            pltpu.VMEM((MIX_ROWS, 3 * d), _F32),
            pltpu.VMEM((MIX_ROWS, d), _BF16),
        ],
        compiler_params=pltpu.CompilerParams(
            dimension_semantics=("arbitrary", "arbitrary"), vmem_limit_bytes=VMEM_LIMIT),
        name="hybrid_mixer",
    )(x, *weights)


def kernel(x, norm_ffn1_g, ffn1_w_gate, ffn1_w_up, ffn1_w_down, norm_mix_g, w_in, conv_dw, conv_b,
           conv_ln_g, conv_ln_b, w_pa, short_dw, w_pb, pool_w, pool_scale, w_pc, w_o, norm_ffn2_g,
           ffn2_w_gate, ffn2_w_up, ffn2_w_down, final_norm_g):
    batch, seq, d_model = x.shape
    depth = w_in.shape[0]
    assert seq % MIX_ROWS == 0 and (batch * seq) % FFN_ROWS == 0
    assert MIX_ROWS % (MIX_PARTS * CONV_ROW_BLOCK) == 0
    assert MIX_ROWS // MIX_PARTS >= CONV_HALO >= conv_dw.shape[1] - 1
    assert SHORT_HALO >= short_dw.shape[1] - 1
    assert POOL_WINDOWS == (2, 4, 8, 16) and pool_w.shape[1] == len(POOL_WINDOWS)

    def row(v):
        return v.reshape(1, -1)

    def bf(w):
        return w.astype(_BF16)

    def slabs(w):
        k, n = w.shape
        return w.astype(_BF16).reshape(k, n // MIX_COLS, MIX_COLS).transpose(1, 0, 2)

    def sublane_rows(v):
        return jnp.repeat(v, V7X_SUBLANES, axis=0)

    y = x.reshape(batch * seq, d_model)
    fg = row(final_norm_g)
    for l in range(depth):
        y = _ffn(y, row(norm_ffn1_g[l]), bf(ffn1_w_gate[l]), bf(ffn1_w_up[l]), slabs(ffn1_w_down[l]), fg,
                 final_norm=False)
        pool_bd = jax.scipy.linalg.block_diag(*[pool_w[l, i] for i in range(pool_w.shape[1])])
        y = _mixer(y, seq, row(norm_mix_g[l]), slabs(w_in[l]), sublane_rows(conv_dw[l]),
                   sublane_rows(row(conv_b[l])), sublane_rows(row(conv_ln_g[l])),
                   sublane_rows(row(conv_ln_b[l])), slabs(w_pa[l]), short_dw[l], slabs(w_pb[l]),
                   bf(pool_bd), row(pool_scale[l]), slabs(w_pc[l]), slabs(w_o[l]))
        y = _ffn(y, row(norm_ffn2_g[l]), bf(ffn2_w_gate[l]), bf(ffn2_w_up[l]), slabs(ffn2_w_down[l]), fg,
                 final_norm=(l == depth - 1))
    return y.reshape(batch, seq, d_model)
```

```python
import functools

import jax
import jax.numpy as jnp
from jax.experimental import pallas as pl
from jax.experimental.pallas import tpu as pltpu

EPS = 1e-6
POOL_WINDOWS = (2, 4, 8, 16)

V7X_SUBLANES = 8

FFN_ROWS = 1024
FFN_COLS = 256
MIX_ROWS = 512
MIX_PARTS = 2
MIX_COLS = 256
CONV_ROW_BLOCK = 32
CONV_HALO = 32
SHORT_HALO = 8
POOL_HALO = 24
VMEM_LIMIT = 56 * 1024 * 1024

_F32 = jnp.float32
_BF16 = jnp.bfloat16


def _rms_norm(x, g):
    return x * jax.lax.rsqrt(jnp.mean(x * x, axis=-1, keepdims=True) + EPS) * g


def _dot(a, b):
    return jnp.dot(a, b, preferred_element_type=_F32)


def _resident(shape):
    return pl.BlockSpec(shape, lambda *_: (0,) * len(shape), pipeline_mode=pl.Buffered(1))


def _column_slabs(w, width=256):
    k, n = w.shape
    specs = [pl.BlockSpec((k, width), lambda *_, j=j: (0, j), pipeline_mode=pl.Buffered(1))
             for j in range(n // width)]
    return [w] * len(specs), specs


def _interleave(primary, secondary):
    done = 0
    for i, unit in enumerate(primary):
        unit()
        upto = ((i + 1) * len(secondary)) // len(primary)
        for other in secondary[done:upto]:
            other()
        done = upto


def _ffn_kernel(x_ref, g_ref, wg_ref, wu_ref, *refs, final_norm):
    *wd_refs, fg_ref, o_ref, act_ref = refs
    x = x_ref[...]
    h = _rms_norm(x, g_ref[...]).astype(_BF16)
    d_ff = wg_ref.shape[1]
    for c in range(0, d_ff, FFN_COLS):
        gate = _dot(h, wg_ref[:, c:c + FFN_COLS])
        up = _dot(h, wu_ref[:, c:c + FFN_COLS])
        act_ref[:, c:c + FFN_COLS] = (gate * jax.nn.sigmoid(gate) * up).astype(_BF16)
    act = act_ref[...]
    out = x + 0.5 * jnp.concatenate([_dot(act, wd_ref[...]) for wd_ref in wd_refs], axis=1)
    if final_norm:
        out = _rms_norm(out, fg_ref[...])
    o_ref[...] = out


def _ffn(x, g, wg, wu, wd, fg, *, final_norm):
    m, d = x.shape
    d_ff = wg.shape[1]
    wd_slabs, wd_specs = _column_slabs(wd)
    return pl.pallas_call(
        functools.partial(_ffn_kernel, final_norm=final_norm),
        grid=(m // FFN_ROWS,),
        in_specs=[
            pl.BlockSpec((FFN_ROWS, d), lambda i: (i, 0)),
            _resident((1, d)),
            _resident((d, d_ff)),
            _resident((d, d_ff)),
            *wd_specs,
            _resident((1, d)),
        ],
        out_specs=pl.BlockSpec((FFN_ROWS, d), lambda i: (i, 0)),
        out_shape=jax.ShapeDtypeStruct((m, d), _F32),
        scratch_shapes=[pltpu.VMEM((FFN_ROWS, d_ff), _BF16)],
        compiler_params=pltpu.CompilerParams(
            dimension_semantics=("arbitrary",), vmem_limit_bytes=VMEM_LIMIT),
        name="swiglu_half_step",
    )(x, g, wg, wu, *wd_slabs, fg)


def _mixer_kernel(*refs, n_in_slabs, n_out_slabs):
    refs = list(refs)

    def take(n):
        head = refs[:n]
        del refs[:n]
        return head

    x_ref, g_ref = take(2)
    win_ref = take(n_in_slabs)
    cdw_ref, cb_ref, lng_ref, lnb_ref = take(4)
    wpa_ref = take(n_out_slabs)
    sdw_ref, = take(1)
    wpb_ref = take(n_out_slabs)
    pw_ref, ps_ref = take(2)
    wpc_ref = take(n_out_slabs)
    wo_ref = take(n_out_slabs)
    (o_ref, h_ref, u_ref, a_buf, a_shift, a_act_ref, v_buf, b_act_ref,
     p_buf, s2_buf, s4_buf, s8_buf, c_act_ref, gate_ref, merged_ref) = refs
    rows, d_model = x_ref.shape
    part = rows // MIX_PARTS
    d_conv = wpa_ref[0].shape[0]
    d_short = wpb_ref[0].shape[0]
    d_pool = wpc_ref[0].shape[0]
    sub = V7X_SUBLANES
    step = pl.program_id(1)

    @pl.when(step == 0)
    def _():
        a_buf[0:CONV_HALO, :] = jnp.zeros((CONV_HALO, d_conv), _F32)
        v_buf[0:SHORT_HALO, :] = jnp.zeros((SHORT_HALO, d_short), _F32)
        p_buf[0:POOL_HALO, :] = jnp.zeros((POOL_HALO, d_pool), _F32)
        s2_buf[0:sub, :] = jnp.zeros((sub, d_pool), _F32)
        s4_buf[0:sub, :] = jnp.zeros((sub, d_pool), _F32)

    c_aval = 0
    c_agate = c_aval + d_conv
    c_bgate = c_agate + d_conv
    c_cgate = c_bgate + d_short
    c_bx = c_cgate + d_short
    c_pin = c_bx + d_short
    c_gates = c_pin + d_pool

    def proj(o, c0):
        return _dot(h_ref[o:o + part, :], win_ref[c0 // MIX_COLS][...])

    def project_units(o):
        def norm():
            h_ref[o:o + part, :] = _rms_norm(x_ref[o:o + part, :], g_ref[...]).astype(_BF16)

        def mix_in(c):
            u_ref[o:o + part, c:c + MIX_COLS] = proj(o, c)

        def gate_logits(c):
            gate_ref[o:o + part, c:c + MIX_COLS] = proj(o, c_gates + c)

        mix = [functools.partial(mix_in, c) for c in range(0, c_gates, MIX_COLS)]
        gates = [functools.partial(gate_logits, c) for c in range(0, gate_ref.shape[1], MIX_COLS)]
        return [norm] + mix, gates

    k_width = cdw_ref.shape[0] // sub
    first = CONV_HALO - (k_width - 1)
    tiles = CONV_ROW_BLOCK // sub
    span = a_shift.shape[1]

    def mixing_units(o):
        rws = slice(o, o + part)

        def glu():
            a_buf[CONV_HALO + o:CONV_HALO + o + part, :] = (
                u_ref[rws, c_aval:c_aval + d_conv] * jax.nn.sigmoid(u_ref[rws, c_agate:c_agate + d_conv]))

        def shift(r):
            a_shift[r - 1, :, :] = a_buf[pl.ds(o + r, span), :]

        def conv_block(rb):
            acc = jnp.zeros((tiles, sub, d_conv), _F32)
            for k in range(k_width):
                q, r = divmod(first + k, sub)
                lo = rb + q * sub
                if r == 0:
                    tap = a_buf[o + lo:o + lo + CONV_ROW_BLOCK, :]
                else:
                    tap = a_shift[r - 1, lo:lo + CONV_ROW_BLOCK, :]
                acc = acc + cdw_ref[k * sub:(k + 1) * sub, :][None] * tap.reshape(tiles, sub, d_conv)
            acc = acc + cb_ref[...][None]
            xc = acc - jnp.mean(acc, axis=-1, keepdims=True)
            ln = xc * jax.lax.rsqrt(jnp.mean(xc * xc, axis=-1, keepdims=True) + EPS)
            ln = ln * lng_ref[...][None] + lnb_ref[...][None]
            act = (ln * jax.nn.sigmoid(ln)).reshape(CONV_ROW_BLOCK, d_conv)
            a_act_ref[o + rb:o + rb + CONV_ROW_BLOCK, :] = act.astype(_BF16)

        def short_conv():
            s_width = sdw_ref.shape[0]
            v_buf[SHORT_HALO + o:SHORT_HALO + o + part, :] = (
                u_ref[rws, c_cgate:c_cgate + d_short] * u_ref[rws, c_bx:c_bx + d_short])
            s_conv = jnp.zeros((part, d_short), _F32)
            for k in range(s_width):
                s_conv = s_conv + sdw_ref[k:k + 1, :] * v_buf[pl.ds(SHORT_HALO + o - (s_width - 1) + k, part), :]
            b_act_ref[rws, :] = (u_ref[rws, c_bgate:c_bgate + d_short] * s_conv).astype(_BF16)

        def pool():
            p_in = u_ref[rws, c_pin:c_pin + d_pool]
            p_buf[POOL_HALO + o:POOL_HALO + o + part, :] = p_in
            lo = sub if o == 0 else POOL_HALO + o
            n_ext = POOL_HALO + o + part - lo
            ext = slice(lo, lo + n_ext)
            s2_buf[ext, :] = p_buf[ext, :] + p_buf[pl.ds(lo - 1, n_ext), :]
            s4_buf[ext, :] = s2_buf[ext, :] + s2_buf[pl.ds(lo - 2, n_ext), :]
            s8_buf[ext, :] = s4_buf[ext, :] + s4_buf[pl.ds(lo - 4, n_ext), :]
            cur = slice(POOL_HALO + o, POOL_HALO + o + part)
            s8 = s8_buf[cur, :]
            s16 = s8 + s8_buf[POOL_HALO + o - 8:POOL_HALO + o - 8 + part, :]
            group = jax.lax.broadcasted_iota(jnp.int32, (part, d_pool), 1) // (d_pool // len(POOL_WINDOWS))
            wsum = jnp.where(group == 0, s2_buf[cur, :],
                             jnp.where(group == 1, s4_buf[cur, :], jnp.where(group == 2, s8, s16)))
            seen = step * rows + o + 1 + jax.lax.broadcasted_iota(jnp.int32, (part, d_pool), 0)
            count = jnp.minimum(seen, jnp.left_shift(2, group)).astype(_F32)
            pooled = (wsum / count - p_in).astype(_BF16)
            c_act_ref[rws, :] = (_dot(pooled, pw_ref[...]) * ps_ref[...]).astype(_BF16)

        units = [glu] + [functools.partial(shift, r) for r in range(1, sub)]
        units += [functools.partial(conv_block, rb) for rb in range(0, part, CONV_ROW_BLOCK)]
        return units + [short_conv, pool]

    def merge_units(o):
        rws = slice(o, o + part)

        def merge(c):
            cols = slice(c, c + MIX_COLS)

            def gate(branch):
                return jax.nn.sigmoid(gate_ref[rws, branch * d_model + c:branch * d_model + c + MIX_COLS])

            n = c // MIX_COLS
            merged = gate(0) * _dot(a_act_ref[rws, :], wpa_ref[n][...])
            merged = merged + gate(1) * _dot(b_act_ref[rws, :], wpb_ref[n][...])
            merged = merged + gate(2) * _dot(c_act_ref[rws, :], wpc_ref[n][...])
            merged_ref[rws, cols] = merged.astype(_BF16)

        def out(c):
            cols = slice(c, c + MIX_COLS)
            o_ref[rws, cols] = x_ref[rws, cols] + _dot(merged_ref[rws, :], wo_ref[c // MIX_COLS][...])

        chunks = range(0, d_model, MIX_COLS)
        return [functools.partial(merge, c) for c in chunks] + [functools.partial(out, c) for c in chunks]

    offsets = [i * part for i in range(MIX_PARTS)]
    mix_proj, gate_proj = zip(*[project_units(o) for o in offsets])
    for unit in mix_proj[0]:
        unit()
    for i, o in enumerate(offsets):
        mxu = list(gate_proj[i])
        if i + 1 < MIX_PARTS:
            mxu = mix_proj[i + 1] + mxu
        if i > 0:
            mxu = merge_units(offsets[i - 1]) + mxu
        _interleave(mixing_units(o), mxu)
    for unit in merge_units(offsets[-1]):
        unit()

    a_buf[0:CONV_HALO, :] = a_buf[rows:rows + CONV_HALO, :]
    v_buf[0:SHORT_HALO, :] = v_buf[rows:rows + SHORT_HALO, :]
    p_buf[0:POOL_HALO, :] = p_buf[rows:rows + POOL_HALO, :]


def _mixer(x, seq, g, win, cdw, cb, lng, lnb, wpa, sdw, wpb, pw, ps, wpc, wo):
    m, d = x.shape
    steps = seq // MIX_ROWS
    part = MIX_ROWS // MIX_PARTS
    d_conv, d_short, d_pool = wpa.shape[0], wpb.shape[0], wpc.shape[0]
    d_mix = 2 * d_conv + 3 * d_short + d_pool
    row_spec = pl.BlockSpec((MIX_ROWS, d), lambda b, s: (b * steps + s, 0))
    weights, specs = [], []
    for w in (g, win, cdw, cb, lng, lnb, wpa, sdw, wpb, pw, ps, wpc, wo):
        if any(w is big for big in (win, wpa, wpb, wpc, wo)):
            slabs, slab_specs = _column_slabs(w, MIX_COLS)
            weights += slabs
            specs += slab_specs
        else:
            weights.append(w)
            specs.append(_resident(w.shape))
    pool_rows = POOL_HALO + MIX_ROWS
    return pl.pallas_call(
        functools.partial(_mixer_kernel, n_in_slabs=win.shape[1] // MIX_COLS, n_out_slabs=d // MIX_COLS),
        grid=(m // seq, steps),
        in_specs=[row_spec] + specs,
        out_specs=row_spec,
        out_shape=jax.ShapeDtypeStruct((m, d), _F32),
        scratch_shapes=[
            pltpu.VMEM((MIX_ROWS, d), _BF16),
            pltpu.VMEM((MIX_ROWS, d_mix), _F32),
            pltpu.VMEM((CONV_HALO + MIX_ROWS, d_conv), _F32),
            pltpu.VMEM((V7X_SUBLANES - 1, CONV_HALO + part - V7X_SUBLANES, d_conv), _F32),
            pltpu.VMEM((MIX_ROWS, d_conv), _BF16),
            pltpu.VMEM((SHORT_HALO + MIX_ROWS, d_short), _F32),
            pltpu.VMEM((MIX_ROWS, d_short), _BF16),
            pltpu.VMEM((pool_rows, d_pool), _F32),
            pltpu.VMEM((pool_rows, d_pool), _F32),
            pltpu.VMEM((pool_rows, d_pool), _F32),
            pltpu.VMEM((pool_rows, d_pool), _F32),
            pltpu.VMEM((MIX_ROWS, d_pool), _BF16),
            pltpu.VMEM((MIX_ROWS, 3 * d), _F32),
            pltpu.VMEM((MIX_ROWS, d), _BF16),
        ],
        compiler_params=pltpu.CompilerParams(
            dimension_semantics=("arbitrary", "arbitrary"), vmem_limit_bytes=VMEM_LIMIT),
        name="hybrid_mixer",
    )(x, *weights)


def kernel(x, norm_ffn1_g, ffn1_w_gate, ffn1_w_up, ffn1_w_down, norm_mix_g, w_in, conv_dw, conv_b,
           conv_ln_g, conv_ln_b, w_pa, short_dw, w_pb, pool_w, pool_scale, w_pc, w_o, norm_ffn2_g,
           ffn2_w_gate, ffn2_w_up, ffn2_w_down, final_norm_g):
    batch, seq, d_model = x.shape
    depth = w_in.shape[0]
    assert seq % MIX_ROWS == 0 and (batch * seq) % FFN_ROWS == 0
    assert MIX_ROWS % (MIX_PARTS * CONV_ROW_BLOCK) == 0
    assert MIX_ROWS // MIX_PARTS >= CONV_HALO >= conv_dw.shape[1] - 1
    assert SHORT_HALO >= short_dw.shape[1] - 1
    assert POOL_WINDOWS == (2, 4, 8, 16) and pool_w.shape[1] == len(POOL_WINDOWS)

    def row(v):
        return v.reshape(1, -1)

    def bf(w):
        return w.astype(_BF16)

    def sublane_rows(v):
        return jnp.repeat(v, V7X_SUBLANES, axis=0)

    y = x.reshape(batch * seq, d_model)
    fg = row(final_norm_g)
    for l in range(depth):
        y = _ffn(y, row(norm_ffn1_g[l]), bf(ffn1_w_gate[l]), bf(ffn1_w_up[l]), bf(ffn1_w_down[l]), fg,
                 final_norm=False)
        pool_bd = jax.scipy.linalg.block_diag(*[pool_w[l, i] for i in range(pool_w.shape[1])])
        y = _mixer(y, seq, row(norm_mix_g[l]), bf(w_in[l]), sublane_rows(conv_dw[l]),
                   sublane_rows(row(conv_b[l])), sublane_rows(row(conv_ln_g[l])),
                   sublane_rows(row(conv_ln_b[l])), bf(w_pa[l]), short_dw[l], bf(w_pb[l]),
                   bf(pool_bd), row(pool_scale[l]), bf(w_pc[l]), bf(w_o[l]))
        y = _ffn(y, row(norm_ffn2_g[l]), bf(ffn2_w_gate[l]), bf(ffn2_w_up[l]), bf(ffn2_w_down[l]), fg,
                 final_norm=(l == depth - 1))
    return y.reshape(batch, seq, d_model)
```

```python
import functools

import jax
import jax.numpy as jnp
from jax.experimental import pallas as pl
from jax.experimental.pallas import tpu as pltpu

EPS = 1e-6
POOL_WINDOWS = (2, 4, 8, 16)

V7X_SUBLANES = 8

W_COLS = 256
FFN_ROWS = 1024
MIX_ROWS = 512
MIX_PARTS = 2
CONV_ROW_BLOCK = 32
CONV_HALO = 32
SHORT_HALO = 8
POOL_HALO = 24
VMEM_LIMIT = 56 * 1024 * 1024

_F32 = jnp.float32
_BF16 = jnp.bfloat16


def _rms_norm(x, g):
    return x * jax.lax.rsqrt(jnp.mean(x * x, axis=-1, keepdims=True) + EPS) * g


def _dot(a, b):
    return jnp.dot(a, b, preferred_element_type=_F32)


def _resident(shape):
    return pl.BlockSpec(shape, lambda *_: (0,) * len(shape), pipeline_mode=pl.Buffered(1))


_HBM = pl.BlockSpec(memory_space=pl.ANY)


def _block_shapes(k, n):
    return [pltpu.VMEM((k, W_COLS), _BF16) for _ in range(n // W_COLS)]


def _fetch_weights(matrices, staging, sem):
    jobs = [(src.at[:, pl.ds(j * W_COLS, W_COLS)], dst)
            for src, blocks in matrices for j, dst in enumerate(blocks)]

    def copy(i):
        src, dst = jobs[i]
        return pltpu.make_async_copy(src, staging.at[i % 2, pl.ds(0, dst.shape[0]), :], sem.at[i % 2])

    copy(0).start()
    for i, (_, dst) in enumerate(jobs):
        if i + 1 < len(jobs):
            copy(i + 1).start()
        copy(i).wait()
        dst[...] = staging[i % 2, 0:dst.shape[0], :].astype(_BF16)


def _interleave(primary, secondary):
    done = 0
    for i, unit in enumerate(primary):
        unit()
        upto = ((i + 1) * len(secondary)) // len(primary)
        for other in secondary[done:upto]:
            other()
        done = upto


def _ffn_kernel(x_ref, g_ref, wg_hbm, wu_hbm, wd_hbm, fg_ref, o_ref, act_ref, staging, sem, *blocks,
                layer, final_norm):
    d_model, d_ff = wg_hbm.shape[1:]
    n_up, n_down = d_ff // W_COLS, d_model // W_COLS
    wg_blocks, wu_blocks, wd_blocks = blocks[:n_up], blocks[n_up:2 * n_up], blocks[2 * n_up:]
    assert len(wd_blocks) == n_down

    @pl.when(pl.program_id(0) == 0)
    def _():
        _fetch_weights([(wg_hbm.at[layer], wg_blocks), (wu_hbm.at[layer], wu_blocks),
                        (wd_hbm.at[layer], wd_blocks)], staging, sem)

    x = x_ref[...]
    h = _rms_norm(x, g_ref[...]).astype(_BF16)
    for j in range(n_up):
        gate = _dot(h, wg_blocks[j][...])
        up = _dot(h, wu_blocks[j][...])
        act_ref[:, j * W_COLS:(j + 1) * W_COLS] = (gate * jax.nn.sigmoid(gate) * up).astype(_BF16)
    act = act_ref[...]
    out = x + 0.5 * jnp.concatenate([_dot(act, wd[...]) for wd in wd_blocks], axis=1)
    if final_norm:
        out = _rms_norm(out, fg_ref[...])
    o_ref[...] = out


def _ffn(x, g, wg, wu, wd, fg, *, layer, final_norm):
    m, d = x.shape
    d_ff = wg.shape[2]
    return pl.pallas_call(
        functools.partial(_ffn_kernel, layer=layer, final_norm=final_norm),
        grid=(m // FFN_ROWS,),
        in_specs=[
            pl.BlockSpec((FFN_ROWS, d), lambda i: (i, 0)),
            _resident((1, d)),
            _HBM, _HBM, _HBM,
            _resident((1, d)),
        ],
        out_specs=pl.BlockSpec((FFN_ROWS, d), lambda i: (i, 0)),
        out_shape=jax.ShapeDtypeStruct((m, d), _F32),
        scratch_shapes=[
            pltpu.VMEM((FFN_ROWS, d_ff), _BF16),
            pltpu.VMEM((2, max(d, d_ff), W_COLS), _F32),
            pltpu.SemaphoreType.DMA((2,)),
            *_block_shapes(d, d_ff), *_block_shapes(d, d_ff), *_block_shapes(d_ff, d),
        ],
        compiler_params=pltpu.CompilerParams(
            dimension_semantics=("arbitrary",), vmem_limit_bytes=VMEM_LIMIT),
        name="swiglu_half_step",
    )(x, g, wg, wu, wd, fg)


def _mixer_kernel(x_ref, g_ref, win_hbm, cdw_ref, cb_ref, lng_ref, lnb_ref, wpa_hbm, sdw_ref, wpb_hbm,
                  pw_ref, ps_ref, wpc_hbm, wo_hbm, o_ref,
                  h_ref, u_ref, a_buf, a_shift, a_act_ref, v_buf, b_act_ref,
                  p_buf, s2_buf, s4_buf, s8_buf, c_act_ref, gate_ref, merged_ref, staging, sem,
                  *blocks, layer):
    rows, d_model = x_ref.shape
    part = rows // MIX_PARTS
    d_conv, d_short, d_pool = wpa_hbm.shape[1], wpb_hbm.shape[1], wpc_hbm.shape[1]
    n_in, n_out = win_hbm.shape[2] // W_COLS, d_model // W_COLS
    blocks = list(blocks)
    win_ref, wpa_ref, wpb_ref, wpc_ref, wo_ref = (
        blocks[:n_in], *[blocks[n_in + i * n_out:n_in + (i + 1) * n_out] for i in range(4)])
    sub = V7X_SUBLANES
    step = pl.program_id(1)

    @pl.when((pl.program_id(0) == 0) & (step == 0))
    def _():
        _fetch_weights([(win_hbm.at[layer], win_ref), (wpa_hbm.at[layer], wpa_ref),
                        (wpb_hbm.at[layer], wpb_ref), (wpc_hbm.at[layer], wpc_ref),
                        (wo_hbm.at[layer], wo_ref)], staging, sem)

    @pl.when(step == 0)
    def _():
        a_buf[0:CONV_HALO, :] = jnp.zeros((CONV_HALO, d_conv), _F32)
        v_buf[0:SHORT_HALO, :] = jnp.zeros((SHORT_HALO, d_short), _F32)
        p_buf[0:POOL_HALO, :] = jnp.zeros((POOL_HALO, d_pool), _F32)
        s2_buf[0:sub, :] = jnp.zeros((sub, d_pool), _F32)
        s4_buf[0:sub, :] = jnp.zeros((sub, d_pool), _F32)

    c_aval = 0
    c_agate = c_aval + d_conv
    c_bgate = c_agate + d_conv
    c_cgate = c_bgate + d_short
    c_bx = c_cgate + d_short
    c_pin = c_bx + d_short
    c_gates = c_pin + d_pool

    def proj(o, c0):
        return _dot(h_ref[o:o + part, :], win_ref[c0 // W_COLS][...])

    def project_units(o):
        def norm():
            h_ref[o:o + part, :] = _rms_norm(x_ref[o:o + part, :], g_ref[...]).astype(_BF16)

        def mix_in(c):
            u_ref[o:o + part, c:c + W_COLS] = proj(o, c)

        def gate_logits(c):
            gate_ref[o:o + part, c:c + W_COLS] = proj(o, c_gates + c)

        mix = [functools.partial(mix_in, c) for c in range(0, c_gates, W_COLS)]
        gates = [functools.partial(gate_logits, c) for c in range(0, gate_ref.shape[1], W_COLS)]
        return [norm] + mix, gates

    k_width = cdw_ref.shape[0] // sub
    first = CONV_HALO - (k_width - 1)
    tiles = CONV_ROW_BLOCK // sub
    span = a_shift.shape[1]

    def mixing_units(o):
        rws = slice(o, o + part)

        def glu():
            a_buf[CONV_HALO + o:CONV_HALO + o + part, :] = (
                u_ref[rws, c_aval:c_aval + d_conv] * jax.nn.sigmoid(u_ref[rws, c_agate:c_agate + d_conv]))

        def shift(r):
            a_shift[r - 1, :, :] = a_buf[pl.ds(o + r, span), :]

        def conv_block(rb):
            acc = jnp.zeros((tiles, sub, d_conv), _F32)
            for k in range(k_width):
                q, r = divmod(first + k, sub)
                lo = rb + q * sub
                if r == 0:
                    tap = a_buf[o + lo:o + lo + CONV_ROW_BLOCK, :]
                else:
                    tap = a_shift[r - 1, lo:lo + CONV_ROW_BLOCK, :]
                acc = acc + cdw_ref[k * sub:(k + 1) * sub, :][None] * tap.reshape(tiles, sub, d_conv)
            acc = acc + cb_ref[...][None]
            xc = acc - jnp.mean(acc, axis=-1, keepdims=True)
            ln = xc * jax.lax.rsqrt(jnp.mean(xc * xc, axis=-1, keepdims=True) + EPS)
            ln = ln * lng_ref[...][None] + lnb_ref[...][None]
            act = (ln * jax.nn.sigmoid(ln)).reshape(CONV_ROW_BLOCK, d_conv)
            a_act_ref[o + rb:o + rb + CONV_ROW_BLOCK, :] = act.astype(_BF16)

        def short_conv():
            s_width = sdw_ref.shape[0]
            v_buf[SHORT_HALO + o:SHORT_HALO + o + part, :] = (
                u_ref[rws, c_cgate:c_cgate + d_short] * u_ref[rws, c_bx:c_bx + d_short])
            s_conv = jnp.zeros((part, d_short), _F32)
            for k in range(s_width):
                s_conv = s_conv + sdw_ref[k:k + 1, :] * v_buf[pl.ds(SHORT_HALO + o - (s_width - 1) + k, part), :]
            b_act_ref[rws, :] = (u_ref[rws, c_bgate:c_bgate + d_short] * s_conv).astype(_BF16)

        def pool():
            p_in = u_ref[rws, c_pin:c_pin + d_pool]
            p_buf[POOL_HALO + o:POOL_HALO + o + part, :] = p_in
            lo = sub if o == 0 else POOL_HALO + o
            n_ext = POOL_HALO + o + part - lo
            ext = slice(lo, lo + n_ext)
            s2_buf[ext, :] = p_buf[ext, :] + p_buf[pl.ds(lo - 1, n_ext), :]
            s4_buf[ext, :] = s2_buf[ext, :] + s2_buf[pl.ds(lo - 2, n_ext), :]
            s8_buf[ext, :] = s4_buf[ext, :] + s4_buf[pl.ds(lo - 4, n_ext), :]
            cur = slice(POOL_HALO + o, POOL_HALO + o + part)
            s8 = s8_buf[cur, :]
            s16 = s8 + s8_buf[POOL_HALO + o - 8:POOL_HALO + o - 8 + part, :]
            group = jax.lax.broadcasted_iota(jnp.int32, (part, d_pool), 1) // (d_pool // len(POOL_WINDOWS))
            wsum = jnp.where(group == 0, s2_buf[cur, :],
                             jnp.where(group == 1, s4_buf[cur, :], jnp.where(group == 2, s8, s16)))
            seen = step * rows + o + 1 + jax.lax.broadcasted_iota(jnp.int32, (part, d_pool), 0)
            count = jnp.minimum(seen, jnp.left_shift(2, group)).astype(_F32)
            pooled = (wsum / count - p_in).astype(_BF16)
            c_act_ref[rws, :] = (_dot(pooled, pw_ref[...]) * ps_ref[...]).astype(_BF16)

        units = [glu] + [functools.partial(shift, r) for r in range(1, sub)]
        units += [functools.partial(conv_block, rb) for rb in range(0, part, CONV_ROW_BLOCK)]
        return units + [short_conv, pool]

    def merge_units(o):
        rws = slice(o, o + part)

        def merge(c):
            cols = slice(c, c + W_COLS)

            def gate(branch):
                return jax.nn.sigmoid(gate_ref[rws, branch * d_model + c:branch * d_model + c + W_COLS])

            n = c // W_COLS
            merged = gate(0) * _dot(a_act_ref[rws, :], wpa_ref[n][...])
            merged = merged + gate(1) * _dot(b_act_ref[rws, :], wpb_ref[n][...])
            merged = merged + gate(2) * _dot(c_act_ref[rws, :], wpc_ref[n][...])
            merged_ref[rws, cols] = merged.astype(_BF16)

        def out(c):
            cols = slice(c, c + W_COLS)
            o_ref[rws, cols] = x_ref[rws, cols] + _dot(merged_ref[rws, :], wo_ref[c // W_COLS][...])

        chunks = range(0, d_model, W_COLS)
        return [functools.partial(merge, c) for c in chunks] + [functools.partial(out, c) for c in chunks]

    offsets = [i * part for i in range(MIX_PARTS)]
    mix_proj, gate_proj = zip(*[project_units(o) for o in offsets])
    for unit in mix_proj[0]:
        unit()
    for i, o in enumerate(offsets):
        mxu = list(gate_proj[i])
        if i + 1 < MIX_PARTS:
            mxu = mix_proj[i + 1] + mxu
        if i > 0:
            mxu = merge_units(offsets[i - 1]) + mxu
        _interleave(mixing_units(o), mxu)
    for unit in merge_units(offsets[-1]):
        unit()

    a_buf[0:CONV_HALO, :] = a_buf[rows:rows + CONV_HALO, :]
    v_buf[0:SHORT_HALO, :] = v_buf[rows:rows + SHORT_HALO, :]
    p_buf[0:POOL_HALO, :] = p_buf[rows:rows + POOL_HALO, :]


def _mixer(x, seq, g, win, cdw, cb, lng, lnb, wpa, sdw, wpb, pw, ps, wpc, wo, *, layer):
    m, d = x.shape
    steps = seq // MIX_ROWS
    part = MIX_ROWS // MIX_PARTS
    d_in = win.shape[2]
    d_conv, d_short, d_pool = wpa.shape[1], wpb.shape[1], wpc.shape[1]
    d_mix = 2 * d_conv + 3 * d_short + d_pool
    row_spec = pl.BlockSpec((MIX_ROWS, d), lambda b, s: (b * steps + s, 0))
    operands = (g, win, cdw, cb, lng, lnb, wpa, sdw, wpb, pw, ps, wpc, wo)
    in_hbm = (win, wpa, wpb, wpc, wo)
    specs = [_HBM if any(w is big for big in in_hbm) else _resident(w.shape) for w in operands]
    pool_rows = POOL_HALO + MIX_ROWS
    return pl.pallas_call(
        functools.partial(_mixer_kernel, layer=layer),
        grid=(m // seq, steps),
        in_specs=[row_spec] + specs,
        out_specs=row_spec,
        out_shape=jax.ShapeDtypeStruct((m, d), _F32),
        scratch_shapes=[
            pltpu.VMEM((MIX_ROWS, d), _BF16),
            pltpu.VMEM((MIX_ROWS, d_mix), _F32),
            pltpu.VMEM((CONV_HALO + MIX_ROWS, d_conv), _F32),
            pltpu.VMEM((V7X_SUBLANES - 1, CONV_HALO + part - V7X_SUBLANES, d_conv), _F32),
            pltpu.VMEM((MIX_ROWS, d_conv), _BF16),
            pltpu.VMEM((SHORT_HALO + MIX_ROWS, d_short), _F32),
            pltpu.VMEM((MIX_ROWS, d_short), _BF16),
            pltpu.VMEM((pool_rows, d_pool), _F32),
            pltpu.VMEM((pool_rows, d_pool), _F32),
            pltpu.VMEM((pool_rows, d_pool), _F32),
            pltpu.VMEM((pool_rows, d_pool), _F32),
            pltpu.VMEM((MIX_ROWS, d_pool), _BF16),
            pltpu.VMEM((MIX_ROWS, 3 * d), _F32),
            pltpu.VMEM((MIX_ROWS, d), _BF16),
            pltpu.VMEM((2, d, W_COLS), _F32),
            pltpu.SemaphoreType.DMA((2,)),
            *_block_shapes(d, d_in), *_block_shapes(d_conv, d), *_block_shapes(d_short, d),
            *_block_shapes(d_pool, d), *_block_shapes(d, d),
        ],
        compiler_params=pltpu.CompilerParams(
            dimension_semantics=("arbitrary", "arbitrary"), vmem_limit_bytes=VMEM_LIMIT),
        name="hybrid_mixer",
    )(x, *operands)


def kernel(x, norm_ffn1_g, ffn1_w_gate, ffn1_w_up, ffn1_w_down, norm_mix_g, w_in, conv_dw, conv_b,
           conv_ln_g, conv_ln_b, w_pa, short_dw, w_pb, pool_w, pool_scale, w_pc, w_o, norm_ffn2_g,
           ffn2_w_gate, ffn2_w_up, ffn2_w_down, final_norm_g):
    batch, seq, d_model = x.shape
    depth = w_in.shape[0]
    assert seq % MIX_ROWS == 0 and (batch * seq) % FFN_ROWS == 0
    assert MIX_ROWS % (MIX_PARTS * CONV_ROW_BLOCK) == 0
    assert MIX_ROWS // MIX_PARTS >= CONV_HALO >= conv_dw.shape[1] - 1
    assert SHORT_HALO >= short_dw.shape[1] - 1
    assert POOL_WINDOWS == (2, 4, 8, 16) and pool_w.shape[1] == len(POOL_WINDOWS)

    def row(v):
        return v.reshape(1, -1)

    def sublane_rows(v):
        return jnp.repeat(v, V7X_SUBLANES, axis=0)

    y = x.reshape(batch * seq, d_model)
    fg = row(final_norm_g)
    for l in range(depth):
        y = _ffn(y, row(norm_ffn1_g[l]), ffn1_w_gate, ffn1_w_up, ffn1_w_down, fg, layer=l, final_norm=False)
        pool_bd = jax.scipy.linalg.block_diag(*[pool_w[l, i] for i in range(pool_w.shape[1])])
        y = _mixer(y, seq, row(norm_mix_g[l]), w_in, sublane_rows(conv_dw[l]),
                   sublane_rows(row(conv_b[l])), sublane_rows(row(conv_ln_g[l])),
                   sublane_rows(row(conv_ln_b[l])), w_pa, short_dw[l], w_pb,
                   pool_bd.astype(_BF16), row(pool_scale[l]), w_pc, w_o, layer=l)
        y = _ffn(y, row(norm_ffn2_g[l]), ffn2_w_gate, ffn2_w_up, ffn2_w_down, fg, layer=l,
                 final_norm=(l == depth - 1))
    return y.reshape(batch, seq, d_model)
```

```python
import functools

import jax
import jax.numpy as jnp
from jax.experimental import pallas as pl
from jax.experimental.pallas import tpu as pltpu

EPS = 1e-6
POOL_WINDOWS = (2, 4, 8, 16)

V7X_SUBLANES = 8

W_COLS = 256
FETCH_ROWS = 1024
FETCH_SLOTS = 4
FFN_ROWS = 1024
MIX_ROWS = 512
MIX_PARTS = 2
CONV_ROW_BLOCK = 32
CONV_HALO = 32
SHORT_HALO = 8
POOL_HALO = 24
VMEM_LIMIT = 56 * 1024 * 1024

_F32 = jnp.float32
_BF16 = jnp.bfloat16


def _rms_norm(x, g):
    return x * jax.lax.rsqrt(jnp.mean(x * x, axis=-1, keepdims=True) + EPS) * g


def _dot(a, b):
    return jnp.dot(a, b, preferred_element_type=_F32)


def _resident(shape):
    return pl.BlockSpec(shape, lambda *_: (0,) * len(shape), pipeline_mode=pl.Buffered(1))


_HBM = pl.BlockSpec(memory_space=pl.ANY)


def _block_shapes(k, n):
    return [pltpu.VMEM((k, W_COLS), _BF16) for _ in range(n // W_COLS)]


_FETCH_SCRATCH = [pltpu.VMEM((FETCH_SLOTS, FETCH_ROWS, W_COLS), _F32), pltpu.SemaphoreType.DMA((FETCH_SLOTS,))]


def _fetch_weights(matrices, staging, sem):
    slots, piece_rows = staging.shape[:2]
    jobs = []
    for src, blocks in matrices:
        k = src.shape[0]
        for j, dst in enumerate(blocks):
            for r0 in range(0, k, piece_rows):
                n = min(piece_rows, k - r0)
                jobs.append((src.at[pl.ds(r0, n), pl.ds(j * W_COLS, W_COLS)], dst.at[pl.ds(r0, n), :], n))

    def copy(i):
        src, _, n = jobs[i]
        return pltpu.make_async_copy(src, staging.at[i % slots, pl.ds(0, n), :], sem.at[i % slots])

    for i in range(min(slots - 1, len(jobs))):
        copy(i).start()
    for i, (_, dst, n) in enumerate(jobs):
        if i + slots - 1 < len(jobs):
            copy(i + slots - 1).start()
        copy(i).wait()
        dst[...] = staging[i % slots, 0:n, :].astype(_BF16)


def _interleave(primary, secondary):
    done = 0
    for i, unit in enumerate(primary):
        unit()
        upto = ((i + 1) * len(secondary)) // len(primary)
        for other in secondary[done:upto]:
            other()
        done = upto


def _ffn_kernel(x_ref, g_ref, wg_hbm, wu_hbm, wd_hbm, fg_ref, o_ref, act_ref, staging, sem, *blocks,
                layer, final_norm):
    d_model, d_ff = wg_hbm.shape[1:]
    n_up, n_down = d_ff // W_COLS, d_model // W_COLS
    wg_blocks, wu_blocks, wd_blocks = blocks[:n_up], blocks[n_up:2 * n_up], blocks[2 * n_up:]
    assert len(wd_blocks) == n_down

    @pl.when(pl.program_id(0) == 0)
    def _():
        _fetch_weights([(wg_hbm.at[layer], wg_blocks), (wu_hbm.at[layer], wu_blocks),
                        (wd_hbm.at[layer], wd_blocks)], staging, sem)

    x = x_ref[...]
    h = _rms_norm(x, g_ref[...]).astype(_BF16)
    for j in range(n_up):
        gate = _dot(h, wg_blocks[j][...])
        up = _dot(h, wu_blocks[j][...])
        act_ref[:, j * W_COLS:(j + 1) * W_COLS] = (gate * jax.nn.sigmoid(gate) * up).astype(_BF16)
    act = act_ref[...]
    out = x + 0.5 * jnp.concatenate([_dot(act, wd[...]) for wd in wd_blocks], axis=1)
    if final_norm:
        out = _rms_norm(out, fg_ref[...])
    o_ref[...] = out


def _ffn(x, g, wg, wu, wd, fg, *, layer, final_norm):
    m, d = x.shape
    d_ff = wg.shape[2]
    return pl.pallas_call(
        functools.partial(_ffn_kernel, layer=layer, final_norm=final_norm),
        grid=(m // FFN_ROWS,),
        in_specs=[
            pl.BlockSpec((FFN_ROWS, d), lambda i: (i, 0)),
            _resident((1, d)),
            _HBM, _HBM, _HBM,
            _resident((1, d)),
        ],
        out_specs=pl.BlockSpec((FFN_ROWS, d), lambda i: (i, 0)),
        out_shape=jax.ShapeDtypeStruct((m, d), _F32),
        scratch_shapes=[
            pltpu.VMEM((FFN_ROWS, d_ff), _BF16),
            *_FETCH_SCRATCH,
            *_block_shapes(d, d_ff), *_block_shapes(d, d_ff), *_block_shapes(d_ff, d),
        ],
        compiler_params=pltpu.CompilerParams(
            dimension_semantics=("arbitrary",), vmem_limit_bytes=VMEM_LIMIT),
        name="swiglu_half_step",
    )(x, g, wg, wu, wd, fg)


def _mixer_kernel(x_ref, g_ref, win_hbm, cdw_ref, cb_ref, lng_ref, lnb_ref, wpa_hbm, sdw_ref, wpb_hbm,
                  pw_ref, ps_ref, wpc_hbm, wo_hbm, o_ref,
                  h_ref, u_ref, a_buf, a_shift, a_act_ref, v_buf, b_act_ref,
                  p_buf, s2_buf, s4_buf, s8_buf, c_act_ref, gate_ref, merged_ref, staging, sem,
                  *blocks, layer):
    rows, d_model = x_ref.shape
    part = rows // MIX_PARTS
    d_conv, d_short, d_pool = wpa_hbm.shape[1], wpb_hbm.shape[1], wpc_hbm.shape[1]
    n_in, n_out = win_hbm.shape[2] // W_COLS, d_model // W_COLS
    blocks = list(blocks)
    win_ref, wpa_ref, wpb_ref, wpc_ref, wo_ref = (
        blocks[:n_in], *[blocks[n_in + i * n_out:n_in + (i + 1) * n_out] for i in range(4)])
    sub = V7X_SUBLANES
    step = pl.program_id(1)

    @pl.when((pl.program_id(0) == 0) & (step == 0))
    def _():
        _fetch_weights([(win_hbm.at[layer], win_ref), (wpa_hbm.at[layer], wpa_ref),
                        (wpb_hbm.at[layer], wpb_ref), (wpc_hbm.at[layer], wpc_ref),
                        (wo_hbm.at[layer], wo_ref)], staging, sem)

    @pl.when(step == 0)
    def _():
        a_buf[0:CONV_HALO, :] = jnp.zeros((CONV_HALO, d_conv), _F32)
        v_buf[0:SHORT_HALO, :] = jnp.zeros((SHORT_HALO, d_short), _F32)
        p_buf[0:POOL_HALO, :] = jnp.zeros((POOL_HALO, d_pool), _F32)
        s2_buf[0:sub, :] = jnp.zeros((sub, d_pool), _F32)
        s4_buf[0:sub, :] = jnp.zeros((sub, d_pool), _F32)

    c_aval = 0
    c_agate = c_aval + d_conv
    c_bgate = c_agate + d_conv
    c_cgate = c_bgate + d_short
    c_bx = c_cgate + d_short
    c_pin = c_bx + d_short
    c_gates = c_pin + d_pool

    def proj(o, c0):
        return _dot(h_ref[o:o + part, :], win_ref[c0 // W_COLS][...])

    def project_units(o):
        def norm():
            h_ref[o:o + part, :] = _rms_norm(x_ref[o:o + part, :], g_ref[...]).astype(_BF16)

        def mix_in(c):
            u_ref[o:o + part, c:c + W_COLS] = proj(o, c)

        def gate_logits(c):
            gate_ref[o:o + part, c:c + W_COLS] = proj(o, c_gates + c)

        mix = [functools.partial(mix_in, c) for c in range(0, c_gates, W_COLS)]
        gates = [functools.partial(gate_logits, c) for c in range(0, gate_ref.shape[1], W_COLS)]
        return [norm] + mix, gates

    k_width = cdw_ref.shape[0] // sub
    first = CONV_HALO - (k_width - 1)
    tiles = CONV_ROW_BLOCK // sub
    span = a_shift.shape[1]

    def mixing_units(o):
        rws = slice(o, o + part)

        def glu():
            a_buf[CONV_HALO + o:CONV_HALO + o + part, :] = (
                u_ref[rws, c_aval:c_aval + d_conv] * jax.nn.sigmoid(u_ref[rws, c_agate:c_agate + d_conv]))

        def shift(r):
            a_shift[r - 1, :, :] = a_buf[pl.ds(o + r, span), :]

        def conv_block(rb):
            acc = jnp.zeros((tiles, sub, d_conv), _F32)
            for k in range(k_width):
                q, r = divmod(first + k, sub)
                lo = rb + q * sub
                if r == 0:
                    tap = a_buf[o + lo:o + lo + CONV_ROW_BLOCK, :]
                else:
                    tap = a_shift[r - 1, lo:lo + CONV_ROW_BLOCK, :]
                acc = acc + cdw_ref[k * sub:(k + 1) * sub, :][None] * tap.reshape(tiles, sub, d_conv)
            acc = acc + cb_ref[...][None]
            xc = acc - jnp.mean(acc, axis=-1, keepdims=True)
            ln = xc * jax.lax.rsqrt(jnp.mean(xc * xc, axis=-1, keepdims=True) + EPS)
            ln = ln * lng_ref[...][None] + lnb_ref[...][None]
            act = (ln * jax.nn.sigmoid(ln)).reshape(CONV_ROW_BLOCK, d_conv)
            a_act_ref[o + rb:o + rb + CONV_ROW_BLOCK, :] = act.astype(_BF16)

        def short_conv():
            s_width = sdw_ref.shape[0]
            v_buf[SHORT_HALO + o:SHORT_HALO + o + part, :] = (
                u_ref[rws, c_cgate:c_cgate + d_short] * u_ref[rws, c_bx:c_bx + d_short])
            s_conv = jnp.zeros((part, d_short), _F32)
            for k in range(s_width):
                s_conv = s_conv + sdw_ref[k:k + 1, :] * v_buf[pl.ds(SHORT_HALO + o - (s_width - 1) + k, part), :]
            b_act_ref[rws, :] = (u_ref[rws, c_bgate:c_bgate + d_short] * s_conv).astype(_BF16)

        def pool():
            p_in = u_ref[rws, c_pin:c_pin + d_pool]
            p_buf[POOL_HALO + o:POOL_HALO + o + part, :] = p_in
            lo = sub if o == 0 else POOL_HALO + o
            n_ext = POOL_HALO + o + part - lo
            ext = slice(lo, lo + n_ext)
            s2_buf[ext, :] = p_buf[ext, :] + p_buf[pl.ds(lo - 1, n_ext), :]
            s4_buf[ext, :] = s2_buf[ext, :] + s2_buf[pl.ds(lo - 2, n_ext), :]
            s8_buf[ext, :] = s4_buf[ext, :] + s4_buf[pl.ds(lo - 4, n_ext), :]
            cur = slice(POOL_HALO + o, POOL_HALO + o + part)
            s8 = s8_buf[cur, :]
            s16 = s8 + s8_buf[POOL_HALO + o - 8:POOL_HALO + o - 8 + part, :]
            group = jax.lax.broadcasted_iota(jnp.int32, (part, d_pool), 1) // (d_pool // len(POOL_WINDOWS))
            wsum = jnp.where(group == 0, s2_buf[cur, :],
                             jnp.where(group == 1, s4_buf[cur, :], jnp.where(group == 2, s8, s16)))
            seen = step * rows + o + 1 + jax.lax.broadcasted_iota(jnp.int32, (part, d_pool), 0)
            count = jnp.minimum(seen, jnp.left_shift(2, group)).astype(_F32)
            pooled = (wsum / count - p_in).astype(_BF16)
            c_act_ref[rws, :] = (_dot(pooled, pw_ref[...]) * ps_ref[...]).astype(_BF16)

        units = [glu] + [functools.partial(shift, r) for r in range(1, sub)]
        units += [functools.partial(conv_block, rb) for rb in range(0, part, CONV_ROW_BLOCK)]
        return units + [short_conv, pool]

    def merge_units(o):
        rws = slice(o, o + part)

        def merge(c):
            cols = slice(c, c + W_COLS)

            def gate(branch):
                return jax.nn.sigmoid(gate_ref[rws, branch * d_model + c:branch * d_model + c + W_COLS])

            n = c // W_COLS
            merged = gate(0) * _dot(a_act_ref[rws, :], wpa_ref[n][...])
            merged = merged + gate(1) * _dot(b_act_ref[rws, :], wpb_ref[n][...])
            merged = merged + gate(2) * _dot(c_act_ref[rws, :], wpc_ref[n][...])
            merged_ref[rws, cols] = merged.astype(_BF16)

        def out(c):
            cols = slice(c, c + W_COLS)
            o_ref[rws, cols] = x_ref[rws, cols] + _dot(merged_ref[rws, :], wo_ref[c // W_COLS][...])

        chunks = range(0, d_model, W_COLS)
        return [functools.partial(merge, c) for c in chunks] + [functools.partial(out, c) for c in chunks]

    offsets = [i * part for i in range(MIX_PARTS)]
    mix_proj, gate_proj = zip(*[project_units(o) for o in offsets])
    for unit in mix_proj[0]:
        unit()
    for i, o in enumerate(offsets):
        mxu = list(gate_proj[i])
        if i + 1 < MIX_PARTS:
            mxu = mix_proj[i + 1] + mxu
        if i > 0:
            mxu = merge_units(offsets[i - 1]) + mxu
        _interleave(mixing_units(o), mxu)
    for unit in merge_units(offsets[-1]):
        unit()

    a_buf[0:CONV_HALO, :] = a_buf[rows:rows + CONV_HALO, :]
    v_buf[0:SHORT_HALO, :] = v_buf[rows:rows + SHORT_HALO, :]
    p_buf[0:POOL_HALO, :] = p_buf[rows:rows + POOL_HALO, :]


def _mixer(x, seq, g, win, cdw, cb, lng, lnb, wpa, sdw, wpb, pw, ps, wpc, wo, *, layer):
    m, d = x.shape
    steps = seq // MIX_ROWS
    part = MIX_ROWS // MIX_PARTS
    d_in = win.shape[2]
    d_conv, d_short, d_pool = wpa.shape[1], wpb.shape[1], wpc.shape[1]
    d_mix = 2 * d_conv + 3 * d_short + d_pool
    row_spec = pl.BlockSpec((MIX_ROWS, d), lambda b, s: (b * steps + s, 0))
    operands = (g, win, cdw, cb, lng, lnb, wpa, sdw, wpb, pw, ps, wpc, wo)
    in_hbm = (win, wpa, wpb, wpc, wo)
    specs = [_HBM if any(w is big for big in in_hbm) else _resident(w.shape) for w in operands]
    pool_rows = POOL_HALO + MIX_ROWS
    return pl.pallas_call(
        functools.partial(_mixer_kernel, layer=layer),
        grid=(m // seq, steps),
        in_specs=[row_spec] + specs,
        out_specs=row_spec,
        out_shape=jax.ShapeDtypeStruct((m, d), _F32),
        scratch_shapes=[
            pltpu.VMEM((MIX_ROWS, d), _BF16),
            pltpu.VMEM((MIX_ROWS, d_mix), _F32),
            pltpu.VMEM((CONV_HALO + MIX_ROWS, d_conv), _F32),
            pltpu.VMEM((V7X_SUBLANES - 1, CONV_HALO + part - V7X_SUBLANES, d_conv), _F32),
            pltpu.VMEM((MIX_ROWS, d_conv), _BF16),
            pltpu.VMEM((SHORT_HALO + MIX_ROWS, d_short), _F32),
            pltpu.VMEM((MIX_ROWS, d_short), _BF16),
            pltpu.VMEM((pool_rows, d_pool), _F32),
            pltpu.VMEM((pool_rows, d_pool), _F32),
            pltpu.VMEM((pool_rows, d_pool), _F32),
            pltpu.VMEM((pool_rows, d_pool), _F32),
            pltpu.VMEM((MIX_ROWS, d_pool), _BF16),
            pltpu.VMEM((MIX_ROWS, 3 * d), _F32),
            pltpu.VMEM((MIX_ROWS, d), _BF16),
            *_FETCH_SCRATCH,
            *_block_shapes(d, d_in), *_block_shapes(d_conv, d), *_block_shapes(d_short, d),
            *_block_shapes(d_pool, d), *_block_shapes(d, d),
        ],
        compiler_params=pltpu.CompilerParams(
            dimension_semantics=("arbitrary", "arbitrary"), vmem_limit_bytes=VMEM_LIMIT),
        name="hybrid_mixer",
    )(x, *operands)


def kernel(x, norm_ffn1_g, ffn1_w_gate, ffn1_w_up, ffn1_w_down, norm_mix_g, w_in, conv_dw, conv_b,
           conv_ln_g, conv_ln_b, w_pa, short_dw, w_pb, pool_w, pool_scale, w_pc, w_o, norm_ffn2_g,
           ffn2_w_gate, ffn2_w_up, ffn2_w_down, final_norm_g):
    batch, seq, d_model = x.shape
    depth = w_in.shape[0]
    assert seq % MIX_ROWS == 0 and (batch * seq) % FFN_ROWS == 0
    assert MIX_ROWS % (MIX_PARTS * CONV_ROW_BLOCK) == 0
    assert MIX_ROWS // MIX_PARTS >= CONV_HALO >= conv_dw.shape[1] - 1
    assert SHORT_HALO >= short_dw.shape[1] - 1
    assert POOL_WINDOWS == (2, 4, 8, 16) and pool_w.shape[1] == len(POOL_WINDOWS)

    def row(v):
        return v.reshape(1, -1)

    def sublane_rows(v):
        return jnp.repeat(v, V7X_SUBLANES, axis=0)

    y = x.reshape(batch * seq, d_model)
    fg = row(final_norm_g)
    for l in range(depth):
        y = _ffn(y, row(norm_ffn1_g[l]), ffn1_w_gate, ffn1_w_up, ffn1_w_down, fg, layer=l, final_norm=False)
        pool_bd = jax.scipy.linalg.block_diag(*[pool_w[l, i] for i in range(pool_w.shape[1])])
        y = _mixer(y, seq, row(norm_mix_g[l]), w_in, sublane_rows(conv_dw[l]),
                   sublane_rows(row(conv_b[l])), sublane_rows(row(conv_ln_g[l])),
                   sublane_rows(row(conv_ln_b[l])), w_pa, short_dw[l], w_pb,
                   pool_bd.astype(_BF16), row(pool_scale[l]), w_pc, w_o, layer=l)
        y = _ffn(y, row(norm_ffn2_g[l]), ffn2_w_gate, ffn2_w_up, ffn2_w_down, fg, layer=l,
                 final_norm=(l == depth - 1))
    return y.reshape(batch, seq, d_model)
```

```python
import functools

import jax
import jax.numpy as jnp
from jax.experimental import pallas as pl
from jax.experimental.pallas import tpu as pltpu

EPS = 1e-6
POOL_WINDOWS = (2, 4, 8, 16)

V7X_SUBLANES = 8

W_COLS = 256
FETCH_ROWS = 1024
FETCH_SLOTS = 4
FFN_ROWS = 1024
MIX_ROWS = 512
MIX_PARTS = 2
CONV_ROW_BLOCK = 32
CONV_HALO = 32
SHORT_HALO = 8
POOL_HALO = 24
V7X_VMEM_BYTES = 64 * 1024 * 1024
COMPILER_VMEM_ALLOWANCE = 8 * 1024 * 1024

_F32 = jnp.float32
_BF16 = jnp.bfloat16


def _rms_norm(x, g):
    return x * jax.lax.rsqrt(jnp.mean(x * x, axis=-1, keepdims=True) + EPS) * g


def _dot(a, b):
    return jnp.dot(a, b, preferred_element_type=_F32)


def _resident(shape):
    return pl.BlockSpec(shape, lambda *_: (0,) * len(shape), pipeline_mode=pl.Buffered(1))


_HBM = pl.BlockSpec(memory_space=pl.ANY)


def _vmem_limit(scratch, windows):
    def nbytes(a):
        size = jnp.dtype(a.dtype).itemsize
        for dim in a.shape:
            size *= dim
        return size

    total = sum(nbytes(s) for s in scratch if s.memory_space == pltpu.VMEM)
    total += sum(nbytes(a) * buffers for a, buffers in windows) + COMPILER_VMEM_ALLOWANCE
    assert total <= V7X_VMEM_BYTES, total
    return total


def _block_shapes(k, n):
    return [pltpu.VMEM((k, W_COLS), _BF16) for _ in range(n // W_COLS)]


_FETCH_SCRATCH = [pltpu.VMEM((FETCH_SLOTS, FETCH_ROWS, W_COLS), _F32), pltpu.SemaphoreType.DMA((FETCH_SLOTS,))]


def _fetch_weights(matrices, staging, sem):
    slots, piece_rows = staging.shape[:2]
    jobs = []
    for src, blocks in matrices:
        k = src.shape[0]
        for j, dst in enumerate(blocks):
            for r0 in range(0, k, piece_rows):
                n = min(piece_rows, k - r0)
                jobs.append((src.at[pl.ds(r0, n), pl.ds(j * W_COLS, W_COLS)], dst.at[pl.ds(r0, n), :], n))

    def copy(i):
        src, _, n = jobs[i]
        return pltpu.make_async_copy(src, staging.at[i % slots, pl.ds(0, n), :], sem.at[i % slots])

    for i in range(min(slots - 1, len(jobs))):
        copy(i).start()
    for i, (_, dst, n) in enumerate(jobs):
        if i + slots - 1 < len(jobs):
            copy(i + slots - 1).start()
        copy(i).wait()
        dst[...] = staging[i % slots, 0:n, :].astype(_BF16)


def _interleave(primary, secondary):
    done = 0
    for i, unit in enumerate(primary):
        unit()
        upto = ((i + 1) * len(secondary)) // len(primary)
        for other in secondary[done:upto]:
            other()
        done = upto


def _ffn_kernel(x_ref, g_ref, wg_hbm, wu_hbm, wd_hbm, fg_ref, o_ref, act_ref, staging, sem, *blocks,
                layer, final_norm):
    d_model, d_ff = wg_hbm.shape[1:]
    n_up, n_down = d_ff // W_COLS, d_model // W_COLS
    wg_blocks, wu_blocks, wd_blocks = blocks[:n_up], blocks[n_up:2 * n_up], blocks[2 * n_up:]
    assert len(wd_blocks) == n_down

    @pl.when(pl.program_id(0) == 0)
    def _():
        _fetch_weights([(wg_hbm.at[layer], wg_blocks), (wu_hbm.at[layer], wu_blocks),
                        (wd_hbm.at[layer], wd_blocks)], staging, sem)

    x = x_ref[...]
    h = _rms_norm(x, g_ref[...]).astype(_BF16)
    for j in range(n_up):
        gate = _dot(h, wg_blocks[j][...])
        up = _dot(h, wu_blocks[j][...])
        act_ref[:, j * W_COLS:(j + 1) * W_COLS] = (gate * jax.nn.sigmoid(gate) * up).astype(_BF16)
    act = act_ref[...]
    out = x + 0.5 * jnp.concatenate([_dot(act, wd[...]) for wd in wd_blocks], axis=1)
    if final_norm:
        out = _rms_norm(out, fg_ref[...])
    o_ref[...] = out


def _ffn(x, g, wg, wu, wd, fg, *, layer, final_norm):
    m, d = x.shape
    d_ff = wg.shape[2]
    scratch = [
        pltpu.VMEM((FFN_ROWS, d_ff), _BF16),
        *_FETCH_SCRATCH,
        *_block_shapes(d, d_ff), *_block_shapes(d, d_ff), *_block_shapes(d_ff, d),
    ]
    tile = jax.ShapeDtypeStruct((FFN_ROWS, d), _F32)
    return pl.pallas_call(
        functools.partial(_ffn_kernel, layer=layer, final_norm=final_norm),
        grid=(m // FFN_ROWS,),
        in_specs=[
            pl.BlockSpec((FFN_ROWS, d), lambda i: (i, 0)),
            _resident((1, d)),
            _HBM, _HBM, _HBM,
            _resident((1, d)),
        ],
        out_specs=pl.BlockSpec((FFN_ROWS, d), lambda i: (i, 0)),
        out_shape=jax.ShapeDtypeStruct((m, d), _F32),
        scratch_shapes=scratch,
        compiler_params=pltpu.CompilerParams(
            dimension_semantics=("arbitrary",),
            vmem_limit_bytes=_vmem_limit(scratch, [(tile, 2), (tile, 2), (g, 1), (fg, 1)])),
        name="swiglu_half_step",
    )(x, g, wg, wu, wd, fg)


def _mixer_kernel(x_ref, g_ref, win_hbm, cdw_ref, cb_ref, lng_ref, lnb_ref, wpa_hbm, sdw_ref, wpb_hbm,
                  pw_ref, ps_ref, wpc_hbm, wo_hbm, o_ref,
                  h_ref, u_ref, a_buf, a_shift, a_act_ref, v_buf, b_act_ref,
                  p_buf, s2_buf, s4_buf, s8_buf, c_act_ref, gate_ref, merged_ref, staging, sem,
                  *blocks, layer):
    rows, d_model = x_ref.shape
    part = rows // MIX_PARTS
    d_conv, d_short, d_pool = wpa_hbm.shape[1], wpb_hbm.shape[1], wpc_hbm.shape[1]
    n_in, n_out = win_hbm.shape[2] // W_COLS, d_model // W_COLS
    blocks = list(blocks)
    win_ref, wpa_ref, wpb_ref, wpc_ref, wo_ref = (
        blocks[:n_in], *[blocks[n_in + i * n_out:n_in + (i + 1) * n_out] for i in range(4)])
    sub = V7X_SUBLANES
    step = pl.program_id(1)

    @pl.when((pl.program_id(0) == 0) & (step == 0))
    def _():
        _fetch_weights([(win_hbm.at[layer], win_ref), (wpa_hbm.at[layer], wpa_ref),
                        (wpb_hbm.at[layer], wpb_ref), (wpc_hbm.at[layer], wpc_ref),
                        (wo_hbm.at[layer], wo_ref)], staging, sem)

    @pl.when(step == 0)
    def _():
        a_buf[0:CONV_HALO, :] = jnp.zeros((CONV_HALO, d_conv), _F32)
        v_buf[0:SHORT_HALO, :] = jnp.zeros((SHORT_HALO, d_short), _F32)
        p_buf[0:POOL_HALO, :] = jnp.zeros((POOL_HALO, d_pool), _F32)
        s2_buf[0:sub, :] = jnp.zeros((sub, d_pool), _F32)
        s4_buf[0:sub, :] = jnp.zeros((sub, d_pool), _F32)

    c_aval = 0
    c_agate = c_aval + d_conv
    c_bgate = c_agate + d_conv
    c_cgate = c_bgate + d_short
    c_bx = c_cgate + d_short
    c_pin = c_bx + d_short
    c_gates = c_pin + d_pool

    def proj(o, c0):
        return _dot(h_ref[o:o + part, :], win_ref[c0 // W_COLS][...])

    def project_units(o):
        def norm():
            h_ref[o:o + part, :] = _rms_norm(x_ref[o:o + part, :], g_ref[...]).astype(_BF16)

        def mix_in(c):
            u_ref[o:o + part, c:c + W_COLS] = proj(o, c)

        def gate_logits(c):
            gate_ref[o:o + part, c:c + W_COLS] = proj(o, c_gates + c)

        mix = [functools.partial(mix_in, c) for c in range(0, c_gates, W_COLS)]
        gates = [functools.partial(gate_logits, c) for c in range(0, gate_ref.shape[1], W_COLS)]
        return [norm] + mix, gates

    k_width = cdw_ref.shape[0] // sub
    first = CONV_HALO - (k_width - 1)
    tiles = CONV_ROW_BLOCK // sub
    span = a_shift.shape[1]

    def mixing_units(o):
        rws = slice(o, o + part)

        def glu():
            a_buf[CONV_HALO + o:CONV_HALO + o + part, :] = (
                u_ref[rws, c_aval:c_aval + d_conv] * jax.nn.sigmoid(u_ref[rws, c_agate:c_agate + d_conv]))

        def shift(r):
            a_shift[r - 1, :, :] = a_buf[pl.ds(o + r, span), :]

        def conv_block(rb):
            acc = jnp.zeros((tiles, sub, d_conv), _F32)
            for k in range(k_width):
                q, r = divmod(first + k, sub)
                lo = rb + q * sub
                if r == 0:
                    tap = a_buf[o + lo:o + lo + CONV_ROW_BLOCK, :]
                else:
                    tap = a_shift[r - 1, lo:lo + CONV_ROW_BLOCK, :]
                acc = acc + cdw_ref[k * sub:(k + 1) * sub, :][None] * tap.reshape(tiles, sub, d_conv)
            acc = acc + cb_ref[...][None]
            xc = acc - jnp.mean(acc, axis=-1, keepdims=True)
            ln = xc * jax.lax.rsqrt(jnp.mean(xc * xc, axis=-1, keepdims=True) + EPS)
            ln = ln * lng_ref[...][None] + lnb_ref[...][None]
            act = (ln * jax.nn.sigmoid(ln)).reshape(CONV_ROW_BLOCK, d_conv)
            a_act_ref[o + rb:o + rb + CONV_ROW_BLOCK, :] = act.astype(_BF16)

        def short_conv():
            s_width = sdw_ref.shape[0]
            v_buf[SHORT_HALO + o:SHORT_HALO + o + part, :] = (
                u_ref[rws, c_cgate:c_cgate + d_short] * u_ref[rws, c_bx:c_bx + d_short])
            s_conv = jnp.zeros((part, d_short), _F32)
            for k in range(s_width):
                s_conv = s_conv + sdw_ref[k:k + 1, :] * v_buf[pl.ds(SHORT_HALO + o - (s_width - 1) + k, part), :]
            b_act_ref[rws, :] = (u_ref[rws, c_bgate:c_bgate + d_short] * s_conv).astype(_BF16)

        def pool():
            p_in = u_ref[rws, c_pin:c_pin + d_pool]
            p_buf[POOL_HALO + o:POOL_HALO + o + part, :] = p_in
            lo = sub if o == 0 else POOL_HALO + o
            n_ext = POOL_HALO + o + part - lo
            ext = slice(lo, lo + n_ext)
            s2_buf[ext, :] = p_buf[ext, :] + p_buf[pl.ds(lo - 1, n_ext), :]
            s4_buf[ext, :] = s2_buf[ext, :] + s2_buf[pl.ds(lo - 2, n_ext), :]
            s8_buf[ext, :] = s4_buf[ext, :] + s4_buf[pl.ds(lo - 4, n_ext), :]
            cur = slice(POOL_HALO + o, POOL_HALO + o + part)
            s8 = s8_buf[cur, :]
            s16 = s8 + s8_buf[POOL_HALO + o - 8:POOL_HALO + o - 8 + part, :]
            group = jax.lax.broadcasted_iota(jnp.int32, (part, d_pool), 1) // (d_pool // len(POOL_WINDOWS))
            wsum = jnp.where(group == 0, s2_buf[cur, :],
                             jnp.where(group == 1, s4_buf[cur, :], jnp.where(group == 2, s8, s16)))
            seen = step * rows + o + 1 + jax.lax.broadcasted_iota(jnp.int32, (part, d_pool), 0)
            count = jnp.minimum(seen, jnp.left_shift(2, group)).astype(_F32)
            pooled = (wsum / count - p_in).astype(_BF16)
            c_act_ref[rws, :] = (_dot(pooled, pw_ref[...]) * ps_ref[...]).astype(_BF16)

        units = [glu] + [functools.partial(shift, r) for r in range(1, sub)]
        units += [functools.partial(conv_block, rb) for rb in range(0, part, CONV_ROW_BLOCK)]
        return units + [short_conv, pool]

    def merge_units(o):
        rws = slice(o, o + part)

        def merge(c):
            cols = slice(c, c + W_COLS)

            def gate(branch):
                return jax.nn.sigmoid(gate_ref[rws, branch * d_model + c:branch * d_model + c + W_COLS])

            n = c // W_COLS
            merged = gate(0) * _dot(a_act_ref[rws, :], wpa_ref[n][...])
            merged = merged + gate(1) * _dot(b_act_ref[rws, :], wpb_ref[n][...])
            merged = merged + gate(2) * _dot(c_act_ref[rws, :], wpc_ref[n][...])
            merged_ref[rws, cols] = merged.astype(_BF16)

        def out(c):
            cols = slice(c, c + W_COLS)
            o_ref[rws, cols] = x_ref[rws, cols] + _dot(merged_ref[rws, :], wo_ref[c // W_COLS][...])

        chunks = range(0, d_model, W_COLS)
        return [functools.partial(merge, c) for c in chunks] + [functools.partial(out, c) for c in chunks]

    offsets = [i * part for i in range(MIX_PARTS)]
    mix_proj, gate_proj = zip(*[project_units(o) for o in offsets])
    for unit in mix_proj[0]:
        unit()
    for i, o in enumerate(offsets):
        mxu = list(gate_proj[i])
        if i + 1 < MIX_PARTS:
            mxu = mix_proj[i + 1] + mxu
        if i > 0:
            mxu = merge_units(offsets[i - 1]) + mxu
        _interleave(mixing_units(o), mxu)
    for unit in merge_units(offsets[-1]):
        unit()

    a_buf[0:CONV_HALO, :] = a_buf[rows:rows + CONV_HALO, :]
    v_buf[0:SHORT_HALO, :] = v_buf[rows:rows + SHORT_HALO, :]
    p_buf[0:POOL_HALO, :] = p_buf[rows:rows + POOL_HALO, :]


def _mixer(x, seq, g, win, cdw, cb, lng, lnb, wpa, sdw, wpb, pw, ps, wpc, wo, *, layer):
    m, d = x.shape
    steps = seq // MIX_ROWS
    part = MIX_ROWS // MIX_PARTS
    d_in = win.shape[2]
    d_conv, d_short, d_pool = wpa.shape[1], wpb.shape[1], wpc.shape[1]
    d_mix = 2 * d_conv + 3 * d_short + d_pool
    row_spec = pl.BlockSpec((MIX_ROWS, d), lambda b, s: (b * steps + s, 0))
    operands = (g, win, cdw, cb, lng, lnb, wpa, sdw, wpb, pw, ps, wpc, wo)
    in_hbm = (win, wpa, wpb, wpc, wo)
    specs = [_HBM if any(w is big for big in in_hbm) else _resident(w.shape) for w in operands]
    pool_rows = POOL_HALO + MIX_ROWS
    scratch = [
        pltpu.VMEM((MIX_ROWS, d), _BF16),
        pltpu.VMEM((MIX_ROWS, d_mix), _F32),
        pltpu.VMEM((CONV_HALO + MIX_ROWS, d_conv), _F32),
        pltpu.VMEM((V7X_SUBLANES - 1, CONV_HALO + part - V7X_SUBLANES, d_conv), _F32),
        pltpu.VMEM((MIX_ROWS, d_conv), _BF16),
        pltpu.VMEM((SHORT_HALO + MIX_ROWS, d_short), _F32),
        pltpu.VMEM((MIX_ROWS, d_short), _BF16),
        pltpu.VMEM((pool_rows, d_pool), _F32),
        pltpu.VMEM((pool_rows, d_pool), _F32),
        pltpu.VMEM((pool_rows, d_pool), _F32),
        pltpu.VMEM((pool_rows, d_pool), _F32),
        pltpu.VMEM((MIX_ROWS, d_pool), _BF16),
        pltpu.VMEM((MIX_ROWS, 3 * d), _F32),
        pltpu.VMEM((MIX_ROWS, d), _BF16),
        *_FETCH_SCRATCH,
        *_block_shapes(d, d_in), *_block_shapes(d_conv, d), *_block_shapes(d_short, d),
        *_block_shapes(d_pool, d), *_block_shapes(d, d),
    ]
    tile = jax.ShapeDtypeStruct((MIX_ROWS, d), _F32)
    windows = [(tile, 2), (tile, 2)] + [(w, 1) for w in operands if not any(w is big for big in in_hbm)]
    return pl.pallas_call(
        functools.partial(_mixer_kernel, layer=layer),
        grid=(m // seq, steps),
        in_specs=[row_spec] + specs,
        out_specs=row_spec,
        out_shape=jax.ShapeDtypeStruct((m, d), _F32),
        scratch_shapes=scratch,
        compiler_params=pltpu.CompilerParams(
            dimension_semantics=("arbitrary", "arbitrary"), vmem_limit_bytes=_vmem_limit(scratch, windows)),
        name="hybrid_mixer",
    )(x, *operands)


def kernel(x, norm_ffn1_g, ffn1_w_gate, ffn1_w_up, ffn1_w_down, norm_mix_g, w_in, conv_dw, conv_b,
           conv_ln_g, conv_ln_b, w_pa, short_dw, w_pb, pool_w, pool_scale, w_pc, w_o, norm_ffn2_g,
           ffn2_w_gate, ffn2_w_up, ffn2_w_down, final_norm_g):
    batch, seq, d_model = x.shape
    depth = w_in.shape[0]
    assert seq % MIX_ROWS == 0 and (batch * seq) % FFN_ROWS == 0
    assert MIX_ROWS % (MIX_PARTS * CONV_ROW_BLOCK) == 0
    assert MIX_ROWS // MIX_PARTS >= CONV_HALO >= conv_dw.shape[1] - 1
    assert SHORT_HALO >= short_dw.shape[1] - 1
    assert POOL_WINDOWS == (2, 4, 8, 16) and pool_w.shape[1] == len(POOL_WINDOWS)

    def row(v):
        return v.reshape(1, -1)

    def sublane_rows(v):
        return jnp.repeat(v, V7X_SUBLANES, axis=0)

    y = x.reshape(batch * seq, d_model)
    fg = row(final_norm_g)
    for l in range(depth):
        y = _ffn(y, row(norm_ffn1_g[l]), ffn1_w_gate, ffn1_w_up, ffn1_w_down, fg, layer=l, final_norm=False)
        pool_bd = jax.scipy.linalg.block_diag(*[pool_w[l, i] for i in range(pool_w.shape[1])])
        y = _mixer(y, seq, row(norm_mix_g[l]), w_in, sublane_rows(conv_dw[l]),
                   sublane_rows(row(conv_b[l])), sublane_rows(row(conv_ln_g[l])),
                   sublane_rows(row(conv_ln_b[l])), w_pa, short_dw[l], w_pb,
                   pool_bd.astype(_BF16), row(pool_scale[l]), w_pc, w_o, layer=l)
        y = _ffn(y, row(norm_ffn2_g[l]), ffn2_w_gate, ffn2_w_up, ffn2_w_down, fg, layer=l,
                 final_norm=(l == depth - 1))
    return y.reshape(batch, seq, d_model)
```
